```python
import jax, jax.numpy as jnp
from jax import lax
import numpy as np

D_MODEL = 2048
BATCH = 2
SEQ = 16384
DEPTH = 1

GRID_W = 64
WIN_R = 8
WIN_C = 16
ATTN_HEADS = 8
HEAD_DIM = 128
D_ATTN = ATTN_HEADS * HEAD_DIM
D_CONV = D_MODEL // 2
CONV_WIDTH = 31
CONV_PAD = CONV_WIDTH // 2
D_FF = -(-8 * D_MODEL // (3 * 256)) * 256
EPS = 1e-6
IN_SPLITS = [D_ATTN, D_ATTN, D_ATTN, D_CONV, D_CONV, D_MODEL, D_MODEL]
N_IN = sum(IN_SPLITS)

kernel_name = "hybrid_natten_conformer_swiglu"


def rms_norm(x, g):
    xf = x.astype(jnp.float32)
    y = xf * lax.rsqrt(jnp.mean(xf * xf, axis=-1, keepdims=True) + EPS) * g.astype(jnp.float32)
    return y.astype(x.dtype)


def layer_norm(x, g, b):
    xf = x.astype(jnp.float32)
    mu = jnp.mean(xf, axis=-1, keepdims=True)
    xc = xf - mu
    var = jnp.mean(xc * xc, axis=-1, keepdims=True)
    y = xc * lax.rsqrt(var + EPS) * g.astype(jnp.float32) + b.astype(jnp.float32)
    return y.astype(x.dtype)


def neighbourhood_attention(q, k, v, rpb):
    B, S, H, Dh = q.shape
    rows = S // GRID_W
    kr = min(WIN_R, rows)
    kc = min(WIN_C, GRID_W)
    qg = q.reshape(B, rows, GRID_W, H, Dh)
    kg = k.reshape(B, rows, GRID_W, H, Dh)
    vg = v.reshape(B, rows, GRID_W, H, Dh)
    col = jnp.arange(GRID_W)
    col_start = jnp.clip(col - kc // 2, 0, GRID_W - kc)
    col_idx = col_start[:, None] + jnp.arange(kc)[None, :]
    dc = col_idx - col[:, None]
    rpb_cols = rpb[:, :, dc + WIN_C - 1]
    scale = HEAD_DIM ** -0.5

    def one_row(r):
        r_start = jnp.clip(r - kr // 2, 0, rows - kr)
        kb = lax.dynamic_slice_in_dim(kg, r_start, kr, axis=1)
        vb = lax.dynamic_slice_in_dim(vg, r_start, kr, axis=1)
        kw = kb[:, :, col_idx]
        vw = vb[:, :, col_idx]
        qr = lax.dynamic_index_in_dim(qg, r, axis=1, keepdims=False)
        s = jnp.einsum('bwhd,biwjhd->bhwij', qr, kw).astype(jnp.float32) * scale
        dr = r_start + jnp.arange(kr) - r
        bias = jnp.transpose(rpb_cols[:, dr + WIN_R - 1], (0, 2, 1, 3))
        s = s + bias[None].astype(jnp.float32)
        p = jax.nn.softmax(s.reshape(B, H, GRID_W, kr * kc), axis=-1)
        p = p.reshape(B, H, GRID_W, kr, kc).astype(vw.dtype)
        return jnp.einsum('bhwij,biwjhd->bwhd', p, vw)

    out = lax.map(one_row, jnp.arange(rows))
    return jnp.transpose(out, (1, 0, 2, 3, 4)).reshape(B, S, H * Dh)


def conformer_conv(a, b, w_dw, b_dw, ln_g, ln_b, w_pw):
    z = a * jax.nn.sigmoid(b)
    z = lax.conv_general_dilated(
        z, w_dw[:, None, :].astype(z.dtype), window_strides=(1,),
        padding=[(CONV_PAD, CONV_PAD)],
        dimension_numbers=('NWC', 'WIO', 'NWC'),
        feature_group_count=z.shape[-1]) + b_dw
    z = layer_norm(z, ln_g, ln_b)
    z = jax.nn.silu(z)
    return z @ w_pw


def setup_inputs(seed: int = 0) -> dict:
    key = jax.random.key(seed)
    ks = jax.random.split(key, 20)
    f32 = jnp.float32
    nrm = lambda k, shape, s: jax.random.normal(k, shape, f32) * s
    gain = lambda k, n: 1.0 + 0.05 * jax.random.normal(k, (DEPTH, n), f32)
    return {
        'x': jax.random.normal(ks[0], (BATCH, SEQ, D_MODEL), f32),
        'mix_pre_g': gain(ks[1], D_MODEL),
        'mix_post_g': gain(ks[2], D_MODEL),
        'w_in': nrm(ks[3], (DEPTH, D_MODEL, N_IN), D_MODEL ** -0.5),
        'rpb': nrm(ks[4], (DEPTH, ATTN_HEADS, 2 * WIN_R - 1, 2 * WIN_C - 1), 0.1),
        'w_attn_o': nrm(ks[5], (DEPTH, D_ATTN, D_MODEL), D_ATTN ** -0.5),
        'w_dw': nrm(ks[6], (DEPTH, CONV_WIDTH, D_CONV), CONV_WIDTH ** -0.5),
        'b_dw': nrm(ks[7], (DEPTH, D_CONV), 0.02),
        'conv_ln_g': gain(ks[8], D_CONV),
        'conv_ln_b': nrm(ks[9], (DEPTH, D_CONV), 0.02),
        'w_conv_o': nrm(ks[10], (DEPTH, D_CONV, D_MODEL), D_CONV ** -0.5),
        'w_out': nrm(ks[11], (DEPTH, D_MODEL, D_MODEL), D_MODEL ** -0.5),
        'ffn_pre_g': gain(ks[12], D_MODEL),
        'ffn_post_g': gain(ks[13], D_MODEL),
        'w_gate_up': nrm(ks[14], (DEPTH, D_MODEL, 2 * D_FF), D_MODEL ** -0.5),
        'w_down': nrm(ks[15], (DEPTH, D_FF, D_MODEL), D_FF ** -0.5),
    }


def reference(x, mix_pre_g, mix_post_g, w_in, rpb, w_attn_o, w_dw, b_dw,
              conv_ln_g, conv_ln_b, w_conv_o, w_out, ffn_pre_g, ffn_post_g,
              w_gate_up, w_down):
    B, S, D = x.shape
    split_pts = list(np.cumsum(IN_SPLITS)[:-1])
    h = x
    for l in range(DEPTH):
        u = rms_norm(h, mix_pre_g[l])
        proj = u @ w_in[l]
        q, k, v, ca, cb, ga, gb = jnp.split(proj, split_pts, axis=-1)
        q = q.reshape(B, S, ATTN_HEADS, HEAD_DIM)
        k = k.reshape(B, S, ATTN_HEADS, HEAD_DIM)
        v = v.reshape(B, S, ATTN_HEADS, HEAD_DIM)
        y_attn = neighbourhood_attention(q, k, v, rpb[l]) @ w_attn_o[l]
        y_conv = conformer_conv(ca, cb, w_dw[l], b_dw[l], conv_ln_g[l],
                                conv_ln_b[l], w_conv_o[l])
        merged = jax.nn.sigmoid(ga) * y_attn + jax.nn.sigmoid(gb) * y_conv
        h = h + rms_norm(merged @ w_out[l], mix_post_g[l])
        u = rms_norm(h, ffn_pre_g[l])
        gate, up = jnp.split(u @ w_gate_up[l], 2, axis=-1)
        f = (jax.nn.silu(gate) * up) @ w_down[l]
        h = h + rms_norm(f, ffn_post_g[l])
    return h
```

```python
import functools

import jax
import jax.numpy as jnp
import numpy as np
from jax import lax
from jax.experimental import pallas as pl
from jax.experimental.pallas import tpu as pltpu

GRID_W = 64
WIN_R = 8
WIN_C = 16
HEAD_DIM = 128
CONV_WIDTH = 31
CONV_PAD = CONV_WIDTH // 2
EPS = 1e-6
MASK_BIAS = -1e30

LANES = 128
SUBLANES = 8
BF16_ROWS = 16
VMEM_LIMIT_BYTES = 56 * 1024 * 1024

ATTN_ROWS_PER_STEP = 4
CONV_HALO = 16
CONV_CHUNK = 32

BF16 = jnp.bfloat16
F32 = jnp.float32


def _params(*sem):
    return pltpu.CompilerParams(dimension_semantics=sem, vmem_limit_bytes=VMEM_LIMIT_BYTES)


def _pick(n, pref):
    t = min(pref, n)
    while n % t:
        t //= 2
    return t


def _inproj_kernel(x_ref, g_ref, w_ref, o_ref, u_ref, *, q_tiles, q_scale):
    j = pl.program_id(1)

    @pl.when(j == 0)
    def _():
        x = x_ref[...]
        ms = jnp.mean(x * x, axis=-1, keepdims=True)
        u_ref[...] = (x * lax.rsqrt(ms + EPS) * g_ref[...]).astype(BF16)

    acc = jnp.dot(u_ref[...], w_ref[...], preferred_element_type=F32)
    acc = acc * jnp.where(j < q_tiles, q_scale, 1.0).astype(F32)
    o_ref[...] = acc.astype(BF16)


def _inproj(x2, g, w, d_attn):
    t, d = x2.shape
    n = w.shape[1]
    tm = _pick(t, 1024)
    tn = _pick(d_attn, 1024)
    assert n % tn == 0
    kern = functools.partial(_inproj_kernel, q_tiles=d_attn // tn, q_scale=HEAD_DIM ** -0.5)
    return pl.pallas_call(
        kern,
        out_shape=jax.ShapeDtypeStruct((t, n), BF16),
        grid=(t // tm, n // tn),
        in_specs=[
            pl.BlockSpec((tm, d), lambda i, j: (i, 0)),
            pl.BlockSpec((1, d), lambda i, j: (0, 0)),
            pl.BlockSpec((d, tn), lambda i, j: (0, j)),
        ],
        out_specs=pl.BlockSpec((tm, tn), lambda i, j: (i, j)),
        scratch_shapes=[pltpu.VMEM((tm, d), BF16)],
        compiler_params=_params("parallel", "arbitrary"),
        name="inproj",
    )(x2, g.reshape(1, d), w)


def _attn_block_types(rows, r):
    kwr = r + WIN_R - 1
    nb = rows // r
    interior = [i for i in range(nb) if WIN_R // 2 <= i * r <= rows - r - (WIN_R // 2 - 1)]
    assert interior and interior == list(range(interior[0], interior[-1] + 1))
    n_top = interior[0]
    n_bot = nb - 1 - interior[-1]
    type_r0 = [interior[0] * r] + [i * r for i in range(n_top)] + [i * r for i in range(nb - n_bot, nb)]
    return kwr, nb, n_top, n_bot, type_r0


def _attn_bias_indices(rows, r):
    kwr, _, _, _, type_r0 = _attn_block_types(rows, r)
    kr_sz = min(WIN_R, rows)
    kc_sz = min(WIN_C, GRID_W)
    qa = np.arange(r)[:, None, None, None]
    qc = np.arange(GRID_W)[None, :, None, None]
    wi = np.arange(kwr)[None, None, :, None]
    kc = np.arange(GRID_W)[None, None, None, :]
    dr_l, dc_l, ok_l = [], [], []
    for r0 in type_r0:
        ws = int(np.clip(r0 - kr_sz // 2, 0, rows - kwr))
        qr = r0 + qa
        kr = ws + wi
        rs = np.clip(qr - kr_sz // 2, 0, rows - kr_sz)
        cs = np.clip(qc - kc_sz // 2, 0, GRID_W - kc_sz)
        ok = (kr >= rs) & (kr < rs + kr_sz) & (kc >= cs) & (kc < cs + kc_sz)
        dr = np.clip(kr - qr + WIN_R - 1, 0, 2 * WIN_R - 2)
        dc = np.clip(kc - qc + WIN_C - 1, 0, 2 * WIN_C - 2)
        shape = (r * GRID_W, kwr * GRID_W)
        full = (r, GRID_W, kwr, GRID_W)
        dr_l.append(np.broadcast_to(dr, full).reshape(shape))
        dc_l.append(np.broadcast_to(dc, full).reshape(shape))
        ok_l.append(np.broadcast_to(ok, full).reshape(shape))
    return np.stack(dr_l), np.stack(dc_l), np.stack(ok_l)


def _attn_kernel(q_ref, k_ref, v_ref, b_ref, o_ref, *, rows, r, kwr):
    i = pl.program_id(2)
    ws = jnp.clip(i * r - WIN_R // 2, 0, rows - kwr)
    start = pl.multiple_of(ws * GRID_W, GRID_W)
    kw = kwr * GRID_W
    q = q_ref[0]
    k = k_ref[0, pl.ds(start, kw), :]
    v = v_ref[0, pl.ds(start, kw), :]
    s = lax.dot_general(q, k, (((1,), (1,)), ((), ())), preferred_element_type=F32)
    s = s + b_ref[0, 0]
    m = jnp.max(s, axis=-1, keepdims=True)
    p = jnp.exp(s - m)
    l = jnp.sum(p, axis=-1, keepdims=True)
    o = jnp.dot(p.astype(BF16), v, preferred_element_type=F32)
    o_ref[0] = (o / l).astype(BF16)


def _attention(proj3, rpb, d_attn):
    b, s, _ = proj3.shape
    heads = d_attn // HEAD_DIM
    rows = s // GRID_W
    r = ATTN_ROWS_PER_STEP if rows % ATTN_ROWS_PER_STEP == 0 and rows >= 2 * WIN_R else 1
    kwr, nb, n_top, n_bot, _ = _attn_block_types(rows, r)
    dr, dc, ok = _attn_bias_indices(rows, r)
    bias = jnp.where(ok[None], rpb[:, dr, dc], MASK_BIAS).astype(F32)
    rq, kw = r * GRID_W, kwr * GRID_W

    def btype(i):
        return jnp.where(i < n_top, 1 + i, jnp.where(i >= nb - n_bot, 1 + n_top + i - (nb - n_bot), 0))

    kern = functools.partial(_attn_kernel, rows=rows, r=r, kwr=kwr)
    return pl.pallas_call(
        kern,
        out_shape=jax.ShapeDtypeStruct((b, s, d_attn), BF16),
        grid=(b, heads, nb),
        in_specs=[
            pl.BlockSpec((1, rq, HEAD_DIM), lambda bi, h, i: (bi, i, h)),
            pl.BlockSpec((1, s, HEAD_DIM), lambda bi, h, i: (bi, 0, heads + h)),
            pl.BlockSpec((1, s, HEAD_DIM), lambda bi, h, i: (bi, 0, 2 * heads + h)),
            pl.BlockSpec((1, 1, rq, kw), lambda bi, h, i: (h, btype(i), 0, 0)),
        ],
        out_specs=pl.BlockSpec((1, rq, HEAD_DIM), lambda bi, h, i: (bi, i, h)),
        compiler_params=_params("parallel", "parallel", "arbitrary"),
        name="nattn",
    )(proj3, proj3, proj3, bias)


def _conv_kernel(ca_ref, cb_ref, pa_ref, pb_ref, na_ref, nb_ref, w_ref, bdw_ref, lg_ref, lb_ref,
                 o_ref, zr_ref, *, tb, n_blocks):
    i = pl.program_id(1)
    c = o_ref.shape[-1]

    def glu(a_ref, b_ref):
        return a_ref[0].astype(F32) * jax.nn.sigmoid(b_ref[0].astype(F32))

    zr_ref[0, pl.ds(0, CONV_HALO), :] = jnp.where(i > 0, glu(pa_ref, pb_ref), 0.0)
    zr_ref[0, pl.ds(CONV_HALO, tb), :] = glu(ca_ref, cb_ref)
    zr_ref[0, pl.ds(CONV_HALO + tb, CONV_HALO), :] = jnp.where(i < n_blocks - 1, glu(na_ref, nb_ref), 0.0)
    ext = tb + 2 * CONV_HALO
    for sft in range(1, SUBLANES):
        zr_ref[sft, pl.ds(0, ext - SUBLANES), :] = zr_ref[0, pl.ds(sft, ext - SUBLANES), :]

    bdw = bdw_ref[...]
    lg = lg_ref[...]
    lb = lb_ref[...]

    def chunk(ci, carry):
        base = pl.multiple_of(ci * CONV_CHUNK, CONV_CHUNK)
        acc = jnp.zeros((CONV_CHUNK, c), F32)
        for kk in range(CONV_WIDTH):
            off = CONV_HALO - CONV_PAD + kk
            a, sft = divmod(off, SUBLANES)
            zk = zr_ref[sft, pl.ds(pl.multiple_of(base + a * SUBLANES, SUBLANES), CONV_CHUNK), :]
            acc = acc + zk * w_ref[pl.ds(kk, 1), :]
        y = acc + bdw
        mu = jnp.mean(y, axis=-1, keepdims=True)
        yc = y - mu
        var = jnp.mean(yc * yc, axis=-1, keepdims=True)
        yn = yc * lax.rsqrt(var + EPS) * lg + lb
        o_ref[0, pl.ds(base, CONV_CHUNK), :] = (yn * jax.nn.sigmoid(yn)).astype(BF16)
        return carry

    lax.fori_loop(0, tb // CONV_CHUNK, chunk, 0)


def _conv_branch(proj3, w_dw, b_dw, ln_g, ln_b, col_a, d_conv):
    b, s, _ = proj3.shape
    tb = _pick(s, 512)
    assert tb % CONV_CHUNK == 0 and tb % CONV_HALO == 0
    n_blocks = s // tb
    hb = tb // CONV_HALO
    last_h = s // CONV_HALO - 1

    def main(col):
        return pl.BlockSpec((1, tb, d_conv), lambda bi, i: (bi, i, col))

    def prev(col):
        return pl.BlockSpec((1, CONV_HALO, d_conv), lambda bi, i: (bi, jnp.maximum(i * hb - 1, 0), col))

    def nxt(col):
        return pl.BlockSpec((1, CONV_HALO, d_conv), lambda bi, i: (bi, jnp.minimum((i + 1) * hb, last_h), col))

    def vec():
        return pl.BlockSpec((1, d_conv), lambda bi, i: (0, 0))

    kern = functools.partial(_conv_kernel, tb=tb, n_blocks=n_blocks)
    return pl.pallas_call(
        kern,
        out_shape=jax.ShapeDtypeStruct((b, s, d_conv), BF16),
        grid=(b, n_blocks),
        in_specs=[main(col_a), main(col_a + 1), prev(col_a), prev(col_a + 1), nxt(col_a), nxt(col_a + 1),
                  pl.BlockSpec((CONV_WIDTH, d_conv), lambda bi, i: (0, 0)), vec(), vec(), vec()],
        out_specs=pl.BlockSpec((1, tb, d_conv), lambda bi, i: (bi, i, 0)),
        scratch_shapes=[pltpu.VMEM((SUBLANES, tb + 2 * CONV_HALO, d_conv), F32)],
        compiler_params=_params("parallel", "arbitrary"),
        name="convbranch",
    )(proj3, proj3, proj3, proj3, proj3, proj3, w_dw, b_dw.reshape(1, -1), ln_g.reshape(1, -1),
      ln_b.reshape(1, -1))


def _merge_kernel(at_ref, zc_ref, ga0_ref, ga1_ref, gb0_ref, gb1_ref, x_ref, wa_ref, wc_ref, wo_ref, g_ref,
                  o_ref):
    half = ga0_ref.shape[-1]
    ya = jnp.dot(at_ref[...], wa_ref[...], preferred_element_type=F32)
    yc = jnp.dot(zc_ref[...], wc_ref[...], preferred_element_type=F32)

    def gate(ref):
        return jax.nn.sigmoid(ref[...].astype(F32))

    m0 = gate(ga0_ref) * ya[:, :half] + gate(gb0_ref) * yc[:, :half]
    m1 = gate(ga1_ref) * ya[:, half:] + gate(gb1_ref) * yc[:, half:]
    merged = jnp.concatenate([m0, m1], axis=-1).astype(BF16)
    o = jnp.dot(merged, wo_ref[...], preferred_element_type=F32)
    ms = jnp.mean(o * o, axis=-1, keepdims=True)
    o_ref[...] = x_ref[...] + o * lax.rsqrt(ms + EPS) * g_ref[...]


def _merge(attn2, zc2, proj2, x2, w_attn_o, w_conv_o, w_out, g, gate_col):
    t, d = x2.shape
    half = d // 2
    da = attn2.shape[1]
    dc = zc2.shape[1]
    tm = _pick(t, 256)

    def gspec(col):
        return pl.BlockSpec((tm, half), lambda i: (i, col))

    def wspec(shape):
        return pl.BlockSpec(shape, lambda i: (0, 0), pipeline_mode=pl.Buffered(1))

    return pl.pallas_call(
        _merge_kernel,
        out_shape=jax.ShapeDtypeStruct((t, d), F32),
        grid=(t // tm,),
        in_specs=[
            pl.BlockSpec((tm, da), lambda i: (i, 0)),
            pl.BlockSpec((tm, dc), lambda i: (i, 0)),
            gspec(gate_col), gspec(gate_col + 1), gspec(gate_col + 2), gspec(gate_col + 3),
            pl.BlockSpec((tm, d), lambda i: (i, 0)),
            wspec((da, d)), wspec((dc, d)), wspec((d, d)),
            pl.BlockSpec((1, d), lambda i: (0, 0)),
        ],
        out_specs=pl.BlockSpec((tm, d), lambda i: (i, 0)),
        compiler_params=_params("parallel"),
        name="merge",
    )(attn2, zc2, proj2, proj2, proj2, proj2, x2, w_attn_o, w_conv_o, w_out, g.reshape(1, d))


def _ffn_up_kernel(h_ref, g_ref, wg_ref, wu_ref, o_ref, u_ref):
    j = pl.program_id(1)

    @pl.when(j == 0)
    def _():
        x = h_ref[...]
        ms = jnp.mean(x * x, axis=-1, keepdims=True)
        u_ref[...] = (x * lax.rsqrt(ms + EPS) * g_ref[...]).astype(BF16)

    u = u_ref[...]
    gate = jnp.dot(u, wg_ref[...], preferred_element_type=F32)
    up = jnp.dot(u, wu_ref[...], preferred_element_type=F32)
    o_ref[...] = (gate * jax.nn.sigmoid(gate) * up).astype(BF16)


def _ffn_up(h2, g, w_gate_up):
    t, d = h2.shape
    dff = w_gate_up.shape[1] // 2
    tm = _pick(t, 1024)
    tf = _pick(dff, 512)
    nf = dff // tf
    return pl.pallas_call(
        _ffn_up_kernel,
        out_shape=jax.ShapeDtypeStruct((t, dff), BF16),
        grid=(t // tm, nf),
        in_specs=[
            pl.BlockSpec((tm, d), lambda i, j: (i, 0)),
            pl.BlockSpec((1, d), lambda i, j: (0, 0)),
            pl.BlockSpec((d, tf), lambda i, j: (0, j)),
            pl.BlockSpec((d, tf), lambda i, j: (0, nf + j)),
        ],
        out_specs=pl.BlockSpec((tm, tf), lambda i, j: (i, j)),
        scratch_shapes=[pltpu.VMEM((tm, d), BF16)],
        compiler_params=_params("parallel", "arbitrary"),
        name="ffn_up",
    )(h2, g.reshape(1, d), w_gate_up, w_gate_up)


def _ffn_down_kernel(a_ref, w_ref, h_ref, g_ref, o_ref, acc_ref):
    k = pl.program_id(1)

    @pl.when(k == 0)
    def _():
        acc_ref[...] = jnp.zeros_like(acc_ref)

    acc_ref[...] += jnp.dot(a_ref[...], w_ref[...], preferred_element_type=F32)

    @pl.when(k == pl.num_programs(1) - 1)
    def _():
        f = acc_ref[...]
        ms = jnp.mean(f * f, axis=-1, keepdims=True)
        o_ref[...] = h_ref[...] + f * lax.rsqrt(ms + EPS) * g_ref[...]


def _ffn_down(act, w_down, h2, g):
    t, d = h2.shape
    dff = act.shape[1]
    tm = _pick(t, 512)
    tk = _pick(dff, 512)
    return pl.pallas_call(
        _ffn_down_kernel,
        out_shape=jax.ShapeDtypeStruct((t, d), F32),
        grid=(t // tm, dff // tk),
        in_specs=[
            pl.BlockSpec((tm, tk), lambda i, k: (i, k)),
            pl.BlockSpec((tk, d), lambda i, k: (k, 0)),
            pl.BlockSpec((tm, d), lambda i, k: (i, 0)),
            pl.BlockSpec((1, d), lambda i, k: (0, 0)),
        ],
        out_specs=pl.BlockSpec((tm, d), lambda i, k: (i, 0)),
        scratch_shapes=[pltpu.VMEM((tm, d), F32)],
        compiler_params=_params("parallel", "arbitrary"),
        name="ffn_down",
    )(act, w_down, h2, g.reshape(1, d))


def kernel(x, mix_pre_g, mix_post_g, w_in, rpb, w_attn_o, w_dw, b_dw, conv_ln_g, conv_ln_b, w_conv_o, w_out,
           ffn_pre_g, ffn_post_g, w_gate_up, w_down):
    b, s, d = x.shape
    t = b * s
    depth = w_in.shape[0]
    d_attn = w_attn_o.shape[1]
    d_conv = w_conv_o.shape[1]
    assert d_attn == d_conv and w_in.shape[2] == 3 * d_attn + 2 * d_conv + 2 * d and d == 2 * d_conv
    h = x.reshape(t, d)
    for l in range(depth):
        n_in = w_in.shape[2]
        proj = _inproj(h, mix_pre_g[l], w_in[l].astype(BF16), d_attn)
        proj3 = proj.reshape(b, s, n_in)
        attn = _attention(proj3, rpb[l], d_attn)
        zc = _conv_branch(proj3, w_dw[l], b_dw[l], conv_ln_g[l], conv_ln_b[l], 3 * d_attn // d_conv, d_conv)
        gate_col = (3 * d_attn + 2 * d_conv) // (d // 2)
        h = _merge(attn.reshape(t, d_attn), zc.reshape(t, d_conv), proj, h, w_attn_o[l].astype(BF16),
                   w_conv_o[l].astype(BF16), w_out[l].astype(BF16), mix_post_g[l], gate_col)
        act = _ffn_up(h, ffn_pre_g[l], w_gate_up[l].astype(BF16))
        h = _ffn_down(act, w_down[l].astype(BF16), h, ffn_post_g[l])
    return h.reshape(b, s, d)
```

```python
import functools
import math

import jax
import jax.numpy as jnp
import numpy as np
from jax import lax
from jax.experimental import pallas as pl
from jax.experimental.pallas import tpu as pltpu

GRID_W = 64
WIN_R = 8
WIN_C = 16
HEAD_DIM = 128
CONV_WIDTH = 31
CONV_PAD = CONV_WIDTH // 2
EPS = 1e-6
MASK_BIAS = -1e30
LOG2E = math.log2(math.e)

LANES = 128
SUBLANES = 8
VMEM_LIMIT_BYTES = 56 * 1024 * 1024

ATTN_ROWS = 4
ATTN_SUB = 8
CONV_HALO = 16
CONV_CHUNK = 64
NORM_CHUNK = 16
NORM_UNROLL = 8

BF16 = jnp.bfloat16
F32 = jnp.float32


def _params(*sem, flags=None):
    return pltpu.CompilerParams(dimension_semantics=sem, vmem_limit_bytes=VMEM_LIMIT_BYTES, flags=flags)


def _pick(n, pref):
    t = min(pref, n)
    while n % t:
        t //= 2
    return t


def _slabs(ref, n):
    return jnp.concatenate([ref[s].astype(F32) for s in range(n)], axis=-1)


def _inproj_kernel(x_ref, g_ref, w_ref, o_ref, u_ref, *, q_tiles, q_scale):
    j = pl.program_id(1)

    @pl.when(j == 0)
    def _():
        x = x_ref[...]
        ms = jnp.mean(x * x, axis=-1, keepdims=True)
        u_ref[...] = (x * lax.rsqrt(ms + EPS) * g_ref[...]).astype(BF16)

    acc = jnp.dot(u_ref[...], w_ref[...], preferred_element_type=F32)
    acc = acc * jnp.where(j < q_tiles, q_scale, 1.0).astype(F32)
    for s in range(o_ref.shape[0]):
        o_ref[s] = acc[:, s * LANES:(s + 1) * LANES].astype(BF16)


def _inproj(x2, g, w, d_attn):
    t, d = x2.shape
    n = w.shape[1]
    tm = _pick(t, 1024)
    tn = _pick(d_attn, 1024)
    assert n % tn == 0 and tn % LANES == 0
    spt = tn // LANES
    kern = functools.partial(_inproj_kernel, q_tiles=d_attn // tn, q_scale=HEAD_DIM ** -0.5 * LOG2E)
    return pl.pallas_call(
        kern,
        out_shape=jax.ShapeDtypeStruct((n // LANES, t, LANES), BF16),
        grid=(t // tm, n // tn),
        in_specs=[
            pl.BlockSpec((tm, d), lambda i, j: (i, 0)),
            pl.BlockSpec((1, d), lambda i, j: (0, 0)),
            pl.BlockSpec((d, tn), lambda i, j: (0, j)),
        ],
        out_specs=pl.BlockSpec((spt, tm, LANES), lambda i, j: (j, i, 0)),
        scratch_shapes=[pltpu.VMEM((tm, d), BF16)],
        compiler_params=_params("parallel", "arbitrary"),
        name="inproj",
    )(x2, g.reshape(1, d), w)


def _attn_block_types(rows, r):
    kwr = r + WIN_R - 1
    kwr += (kwr * GRID_W) % LANES // GRID_W
    kwr = min(kwr, rows)
    nb = rows // r
    interior = [i for i in range(nb) if WIN_R // 2 <= i * r <= min(rows - kwr + WIN_R // 2,
                                                                  rows - r - (WIN_R // 2 - 1))]
    assert interior and interior == list(range(interior[0], interior[-1] + 1))
    n_top = interior[0]
    n_bot = nb - 1 - interior[-1]
    type_r0 = [interior[0] * r] + [i * r for i in range(n_top)] + [i * r for i in range(nb - n_bot, nb)]
    return kwr, nb, n_top, n_bot, type_r0


def _attn_bias(rpb, rows, r):
    kwr, _, _, _, type_r0 = _attn_block_types(rows, r)
    kr_sz = min(WIN_R, rows)
    kc_sz = min(WIN_C, GRID_W)
    heads, n_dr, _ = rpb.shape
    rpb = rpb.astype(F32) * LOG2E
    pad = jnp.full((heads, n_dr, GRID_W), MASK_BIAS, F32)
    padded = jnp.concatenate([pad, rpb, pad], axis=-1)
    colb = jnp.stack([padded[:, :, GRID_W + WIN_C - 1 - c:2 * GRID_W + WIN_C - 1 - c] for c in range(GRID_W)],
                     axis=2)
    col = np.arange(GRID_W)
    cs = np.clip(col - kc_sz // 2, 0, GRID_W - kc_sz)
    col_ok = (col[None, :] >= cs[:, None]) & (col[None, :] < cs[:, None] + kc_sz)
    colb = jnp.where(col_ok[None, None], colb, MASK_BIAS)
    masked = jnp.full((heads, GRID_W, GRID_W), MASK_BIAS, F32)
    types = []
    for r0 in type_r0:
        ws = int(np.clip(r0 - kr_sz // 2, 0, rows - kwr))
        q_rows = []
        for a in range(r):
            qr = r0 + a
            rs = int(np.clip(qr - kr_sz // 2, 0, rows - kr_sz))
            blocks = []
            for wi in range(kwr):
                kr = ws + wi
                blocks.append(colb[:, kr - qr + WIN_R - 1] if rs <= kr < rs + kr_sz else masked)
            q_rows.append(jnp.concatenate(blocks, axis=2))
        types.append(jnp.concatenate(q_rows, axis=1))
    return jnp.stack(types, axis=1)


def _attn_kernel(q_ref, k_ref, v_ref, b_ref, o_ref, *, rows, r, kwr, sub, nb, n_top, n_bot):
    step = pl.program_id(2)
    rq = r * GRID_W
    kw = kwr * GRID_W
    for sb in range(sub):
        i = step * sub + sb
        ws = jnp.clip(i * r - WIN_R // 2, 0, rows - kwr)
        bt = jnp.where(i < n_top, 1 + i, jnp.where(i >= nb - n_bot, 1 + n_top + i - (nb - n_bot), 0))
        start = pl.multiple_of(ws * GRID_W, GRID_W)
        q = q_ref[0, sb * rq:(sb + 1) * rq, :]
        k = k_ref[0, pl.ds(start, kw), :]
        v = v_ref[0, pl.ds(start, kw), :]
        s = lax.dot_general(q, k, (((1,), (1,)), ((), ())), preferred_element_type=F32)
        s = s + b_ref[0, bt]
        m = jnp.max(s, axis=-1, keepdims=True)
        p = jnp.exp2(s - m)
        l = jnp.sum(p, axis=-1, keepdims=True)
        o = jnp.dot(p.astype(BF16), v, preferred_element_type=F32)
        o_ref[0, sb * rq:(sb + 1) * rq, :] = (o / l).astype(BF16)


def _attention(proj_s, rpb, b, s, d_attn):
    heads = d_attn // HEAD_DIM
    rows = s // GRID_W
    r = ATTN_ROWS if rows % ATTN_ROWS == 0 and rows >= 2 * WIN_R else 1
    kwr, nb, n_top, n_bot, type_r0 = _attn_block_types(rows, r)
    sub = _pick(nb, ATTN_SUB)
    bias = _attn_bias(rpb, rows, r)
    rq, kw = r * GRID_W, kwr * GRID_W
    steps = nb // sub
    kern = functools.partial(_attn_kernel, rows=rows, r=r, kwr=kwr, sub=sub, nb=nb, n_top=n_top, n_bot=n_bot)
    return pl.pallas_call(
        kern,
        out_shape=jax.ShapeDtypeStruct((b, s, d_attn), BF16),
        grid=(b, heads, steps),
        in_specs=[
            pl.BlockSpec((1, rq * sub, HEAD_DIM), lambda bi, h, i: (h, bi * steps + i, 0)),
            pl.BlockSpec((1, s, HEAD_DIM), lambda bi, h, i: (heads + h, bi, 0)),
            pl.BlockSpec((1, s, HEAD_DIM), lambda bi, h, i: (2 * heads + h, bi, 0)),
            pl.BlockSpec((1, len(type_r0), rq, kw), lambda bi, h, i: (h, 0, 0, 0)),
        ],
        out_specs=pl.BlockSpec((1, rq * sub, HEAD_DIM), lambda bi, h, i: (bi, i, h)),
        compiler_params=_params("parallel", "parallel", "arbitrary"),
        name="nattn",
    )(proj_s, proj_s, proj_s, bias)


def _conv_kernel(ca_ref, cb_ref, pa_ref, pb_ref, na_ref, nb_ref, w_ref, bdw_ref, lg_ref, lb_ref,
                 o_ref, zr_ref, y_ref, *, tb, n_blocks):
    i = pl.program_id(1)
    ns = ca_ref.shape[0]
    c = ns * LANES

    def glu(a_ref, b_ref):
        return _slabs(a_ref, ns) * jax.nn.sigmoid(_slabs(b_ref, ns))

    zr_ref[0, pl.ds(0, CONV_HALO), :] = jnp.where(i > 0, glu(pa_ref, pb_ref), 0.0)
    zr_ref[0, pl.ds(CONV_HALO, tb), :] = glu(ca_ref, cb_ref)
    zr_ref[0, pl.ds(CONV_HALO + tb, CONV_HALO), :] = jnp.where(i < n_blocks - 1, glu(na_ref, nb_ref), 0.0)
    ext = tb + 2 * CONV_HALO
    for sft in range(1, SUBLANES):
        zr_ref[sft, pl.ds(0, ext - SUBLANES), :] = zr_ref[0, pl.ds(sft, ext - SUBLANES), :]

    bdw = bdw_ref[...]
    lg = lg_ref[...]
    lb = lb_ref[...]
    groups = CONV_CHUNK // SUBLANES
    max_a = (CONV_HALO - CONV_PAD + CONV_WIDTH - 1) // SUBLANES

    for lt in range(c // LANES):
        lanes = slice(lt * LANES, (lt + 1) * LANES)
        wk = [w_ref[kk, :, lanes] for kk in range(CONV_WIDTH)]

        def taps(ci, carry, lanes=lanes, wk=wk):
            base = pl.multiple_of(ci * CONV_CHUNK, CONV_CHUNK)
            acc = [None] * groups
            for sft in range(SUBLANES):
                for ti in range(groups + max_a):
                    z = None
                    for gi in range(groups):
                        kk = (ti - gi) * SUBLANES + sft - (CONV_HALO - CONV_PAD)
                        if ti - gi < 0 or not 0 <= kk < CONV_WIDTH:
                            continue
                        if z is None:
                            z = zr_ref[sft, pl.ds(pl.multiple_of(base + ti * SUBLANES, SUBLANES), SUBLANES), lanes]
                        term = z * wk[kk]
                        acc[gi] = term if acc[gi] is None else acc[gi] + term
            for gi in range(groups):
                y_ref[pl.ds(pl.multiple_of(base + gi * SUBLANES, SUBLANES), SUBLANES), lanes] = acc[gi]
            return carry

        lax.fori_loop(0, tb // CONV_CHUNK, taps, 0)

    def norm(ci, carry):
        for u in range(NORM_UNROLL):
            base = pl.multiple_of((ci * NORM_UNROLL + u) * NORM_CHUNK, NORM_CHUNK)
            y = y_ref[pl.ds(base, NORM_CHUNK), :] + bdw
            mu = jnp.mean(y, axis=-1, keepdims=True)
            yc = y - mu
            var = jnp.mean(yc * yc, axis=-1, keepdims=True)
            yn = yc * lax.rsqrt(var + EPS) * lg + lb
            o_ref[0, pl.ds(base, NORM_CHUNK), :] = (yn * jax.nn.sigmoid(yn)).astype(BF16)
        return carry

    lax.fori_loop(0, tb // (NORM_CHUNK * NORM_UNROLL), norm, 0)


def _conv_branch(proj_s, w_dw, b_dw, ln_g, ln_b, b, s, slab_a, d_conv):
    tb = _pick(s, 512)
    assert tb % CONV_CHUNK == 0 and tb % (NORM_CHUNK * NORM_UNROLL) == 0 and tb % CONV_HALO == 0
    ns = d_conv // LANES
    assert slab_a % ns == 0
    blk_a = slab_a // ns
    n_blocks = s // tb
    hb = tb // CONV_HALO
    hs = s // CONV_HALO

    def main(blk):
        return pl.BlockSpec((ns, tb, LANES), lambda bi, i: (blk, bi * n_blocks + i, 0))

    def prev(blk):
        return pl.BlockSpec((ns, CONV_HALO, LANES),
                            lambda bi, i: (blk, bi * hs + jnp.maximum(i * hb - 1, 0), 0))

    def nxt(blk):
        return pl.BlockSpec((ns, CONV_HALO, LANES),
                            lambda bi, i: (blk, bi * hs + jnp.minimum((i + 1) * hb, hs - 1), 0))

    def vec():
        return pl.BlockSpec((1, d_conv), lambda bi, i: (0, 0))

    w_b = jnp.broadcast_to(w_dw[:, None, :], (CONV_WIDTH, SUBLANES, d_conv))
    kern = functools.partial(_conv_kernel, tb=tb, n_blocks=n_blocks)
    return pl.pallas_call(
        kern,
        out_shape=jax.ShapeDtypeStruct((b, s, d_conv), BF16),
        grid=(b, n_blocks),
        in_specs=[main(blk_a), main(blk_a + 1), prev(blk_a), prev(blk_a + 1), nxt(blk_a), nxt(blk_a + 1),
                  pl.BlockSpec((CONV_WIDTH, SUBLANES, d_conv), lambda bi, i: (0, 0, 0)), vec(), vec(), vec()],
        out_specs=pl.BlockSpec((1, tb, d_conv), lambda bi, i: (bi, i, 0)),
        scratch_shapes=[pltpu.VMEM((SUBLANES, tb + 2 * CONV_HALO, d_conv), F32),
                        pltpu.VMEM((tb, d_conv), F32)],
        compiler_params=_params("parallel", "arbitrary"),
        name="convbranch",
    )(proj_s, proj_s, proj_s, proj_s, proj_s, proj_s, w_b, b_dw.reshape(1, -1), ln_g.reshape(1, -1),
      ln_b.reshape(1, -1))


def _merge_kernel(*refs, n_gate):
    at_ref, zc_ref = refs[:2]
    ga_refs = refs[2:2 + n_gate]
    gb_refs = refs[2 + n_gate:2 + 2 * n_gate]
    x_ref, wa_ref, wc_ref, wo_ref, g_ref, gn_ref, o_ref, u_ref = refs[2 + 2 * n_gate:]
    ns = ga_refs[0].shape[0]
    ya = jnp.dot(at_ref[...], wa_ref[...], preferred_element_type=F32)
    yc = jnp.dot(zc_ref[...], wc_ref[...], preferred_element_type=F32)
    parts = []
    for p in range(n_gate):
        for s in range(ns):
            c0 = (p * ns + s) * LANES
            ga = jax.nn.sigmoid(ga_refs[p][s].astype(F32))
            gb = jax.nn.sigmoid(gb_refs[p][s].astype(F32))
            parts.append((ga * ya[:, c0:c0 + LANES] + gb * yc[:, c0:c0 + LANES]).astype(BF16))
    merged = jnp.concatenate(parts, axis=-1)
    o = jnp.dot(merged, wo_ref[...], preferred_element_type=F32)
    ms = jnp.mean(o * o, axis=-1, keepdims=True)
    h = x_ref[...] + o * lax.rsqrt(ms + EPS) * g_ref[...]
    o_ref[...] = h
    hs = jnp.mean(h * h, axis=-1, keepdims=True)
    u_ref[...] = (h * lax.rsqrt(hs + EPS) * gn_ref[...]).astype(BF16)


def _merge(attn2, zc2, proj_s, x2, w_attn_o, w_conv_o, w_out, g, g_next, slab_ga):
    t, d = x2.shape
    ns = math.gcd(slab_ga, d // LANES)
    n_gate = d // LANES // ns
    blk = slab_ga // ns
    da = attn2.shape[1]
    dc = zc2.shape[1]
    tm = _pick(t, 256)

    def wspec(shape):
        return pl.BlockSpec(shape, lambda i: (0, 0), pipeline_mode=pl.Buffered(1))

    def gspec(p):
        return pl.BlockSpec((ns, tm, LANES), lambda i: (blk + p, i, 0))

    return pl.pallas_call(
        functools.partial(_merge_kernel, n_gate=n_gate),
        out_shape=(jax.ShapeDtypeStruct((t, d), F32), jax.ShapeDtypeStruct((t, d), BF16)),
        grid=(t // tm,),
        in_specs=[
            pl.BlockSpec((tm, da), lambda i: (i, 0)),
            pl.BlockSpec((tm, dc), lambda i: (i, 0)),
            *[gspec(p) for p in range(2 * n_gate)],
            pl.BlockSpec((tm, d), lambda i: (i, 0)),
            wspec((da, d)), wspec((dc, d)), wspec((d, d)),
            pl.BlockSpec((1, d), lambda i: (0, 0)),
            pl.BlockSpec((1, d), lambda i: (0, 0)),
        ],
        out_specs=(pl.BlockSpec((tm, d), lambda i: (i, 0)), pl.BlockSpec((tm, d), lambda i: (i, 0))),
        compiler_params=_params("parallel"),
        name="merge",
    )(attn2, zc2, *([proj_s] * (2 * n_gate)), x2, w_attn_o, w_conv_o, w_out, g.reshape(1, d),
      g_next.reshape(1, d))


def _ffn_up_kernel(u_ref, wg_ref, wu_ref, o_ref):
    u = u_ref[...]
    gate = jnp.dot(u, wg_ref[...], preferred_element_type=F32)
    up = jnp.dot(u, wu_ref[...], preferred_element_type=F32)
    o_ref[...] = (gate * jax.nn.sigmoid(gate) * up).astype(BF16)


def _ffn_up(u2, w_gate_up):
    t, d = u2.shape
    dff = w_gate_up.shape[1] // 2
    tm = _pick(t, 1024)
    tf = _pick(dff, 512)
    nf = dff // tf
    return pl.pallas_call(
        _ffn_up_kernel,
        out_shape=jax.ShapeDtypeStruct((t, dff), BF16),
        grid=(t // tm, nf),
        in_specs=[
            pl.BlockSpec((tm, d), lambda i, j: (i, 0)),
            pl.BlockSpec((d, tf), lambda i, j: (0, j)),
            pl.BlockSpec((d, tf), lambda i, j: (0, nf + j)),
        ],
        out_specs=pl.BlockSpec((tm, tf), lambda i, j: (i, j)),
        compiler_params=_params("parallel", "arbitrary"),
        name="ffn_up",
    )(u2, w_gate_up, w_gate_up)


def _ffn_down_kernel(a_ref, w_ref, h_ref, g_ref, o_ref):
    k = pl.program_id(1)
    last = pl.num_programs(1) - 1

    def partial_sum():
        return jnp.dot(a_ref[...], w_ref[...], preferred_element_type=F32)

    @pl.when(k == 0)
    def _():
        o_ref[...] = partial_sum()

    @pl.when(jnp.logical_and(k > 0, k < last))
    def _():
        o_ref[...] += partial_sum()

    @pl.when(k == last)
    def _():
        f = o_ref[...] + partial_sum()
        ms = jnp.mean(f * f, axis=-1, keepdims=True)
        o_ref[...] = h_ref[...] + f * lax.rsqrt(ms + EPS) * g_ref[...]


def _ffn_down(act, w_down, h2, g):
    t, d = h2.shape
    dff = act.shape[1]
    tm = _pick(t, 1024)
    tk = _pick(dff, 512)
    assert dff // tk >= 2
    return pl.pallas_call(
        _ffn_down_kernel,
        out_shape=jax.ShapeDtypeStruct((t, d), F32),
        grid=(t // tm, dff // tk),
        in_specs=[
            pl.BlockSpec((tm, tk), lambda i, k: (i, k)),
            pl.BlockSpec((tk, d), lambda i, k: (k, 0)),
            pl.BlockSpec((tm, d), lambda i, k: (i, 0)),
            pl.BlockSpec((1, d), lambda i, k: (0, 0)),
        ],
        out_specs=pl.BlockSpec((tm, d), lambda i, k: (i, 0)),
        compiler_params=_params("parallel", "arbitrary"),
        name="ffn_down",
    )(act, w_down, h2, g.reshape(1, d))


def kernel(x, mix_pre_g, mix_post_g, w_in, rpb, w_attn_o, w_dw, b_dw, conv_ln_g, conv_ln_b, w_conv_o, w_out,
           ffn_pre_g, ffn_post_g, w_gate_up, w_down):
    b, s, d = x.shape
    t = b * s
    depth = w_in.shape[0]
    d_attn = w_attn_o.shape[1]
    d_conv = w_conv_o.shape[1]
    assert d_attn == d_conv and w_in.shape[2] == 3 * d_attn + 2 * d_conv + 2 * d and d == 2 * d_conv
    h = x.reshape(t, d)
    for l in range(depth):
        proj_s = _inproj(h, mix_pre_g[l], w_in[l].astype(BF16), d_attn)
        attn = _attention(proj_s, rpb[l], b, s, d_attn)
        zc = _conv_branch(proj_s, w_dw[l], b_dw[l], conv_ln_g[l], conv_ln_b[l], b, s, 3 * d_attn // LANES,
                          d_conv)
        h, u2 = _merge(attn.reshape(t, d_attn), zc.reshape(t, d_conv), proj_s, h, w_attn_o[l].astype(BF16),
                       w_conv_o[l].astype(BF16), w_out[l].astype(BF16), mix_post_g[l], ffn_pre_g[l],
                       (3 * d_attn + 2 * d_conv) // LANES)
        act = _ffn_up(u2, w_gate_up[l].astype(BF16))
        h = _ffn_down(act, w_down[l].astype(BF16), h, ffn_post_g[l])
    return h.reshape(b, s, d)
```

```python
import functools
import math

import jax
import jax.numpy as jnp
import numpy as np
from jax import lax
from jax.experimental import pallas as pl
from jax.experimental.pallas import tpu as pltpu

GRID_W = 64
WIN_R = 8
WIN_C = 16
HEAD_DIM = 128
CONV_WIDTH = 31
CONV_PAD = CONV_WIDTH // 2
EPS = 1e-6
MASK_BIAS = -1e30
LOG2E = math.log2(math.e)

LANES = 128
SUBLANES = 8
VMEM_LIMIT_BYTES = 56 * 1024 * 1024

ATTN_ROWS = 4
ATTN_SUB = 8
CONV_HALO = 16
NORM_CHUNK = 16

BF16 = jnp.bfloat16
F32 = jnp.float32


def _params(*sem, flags=None):
    return pltpu.CompilerParams(dimension_semantics=sem, vmem_limit_bytes=VMEM_LIMIT_BYTES, flags=flags)


def _pick(n, pref):
    t = min(pref, n)
    while n % t:
        t //= 2
    return t


def _zero_after(v):
    bits = lax.bitcast_convert_type(v, jnp.uint32)
    return lax.bitcast_convert_type((bits >> 16) >> 16, F32)


def _select(j, values):
    out = values[-1]
    for idx in range(len(values) - 2, -1, -1):
        out = jnp.where(j == idx, values[idx], out)
    return out


def _conv_sub_block(ring_ref, wdw_ref, bdw_ref, lg_ref, lb_ref, zc_ref, zr_ref, y_ref, *, c, s, n_sub, bps,
                    chain):
    nl, tm = ring_ref.shape[1], ring_ref.shape[2]
    sr = tm // n_sub
    ext = sr + 2 * CONV_HALO
    slot = c % 2
    first = s == 0
    last = s == n_sub - 1
    row0 = pl.multiple_of(s * sr, sr)
    pslot = jnp.where(first, 1 - slot, slot)
    prow = pl.multiple_of(jnp.where(first, tm - CONV_HALO, row0 - CONV_HALO), CONV_HALO)
    nslot = jnp.where(last, 1 - slot, slot)
    nrow = pl.multiple_of(jnp.where(last, 0, row0 + sr), CONV_HALO)
    pzero = jnp.logical_and(first, c % bps == 0)
    nzero = jnp.logical_and(last, c % bps == bps - 1)

    def glu(sl, row, nrows, lt):
        a = ring_ref[2 * sl, lt, pl.ds(row, nrows), :].astype(F32)
        b = ring_ref[2 * sl + 1, lt, pl.ds(row, nrows), :].astype(F32)
        return a * jax.nn.sigmoid(b)

    for lt in range(nl):
        lanes = slice(lt * LANES, (lt + 1) * LANES)
        zr_ref[0, 0:CONV_HALO, lanes] = jnp.where(pzero, 0.0, glu(pslot, prow, CONV_HALO, lt))
        zr_ref[0, CONV_HALO:CONV_HALO + sr, lanes] = glu(slot, row0, sr, lt)
        zr_ref[0, CONV_HALO + sr:ext, lanes] = jnp.where(nzero, 0.0, glu(nslot, nrow, CONV_HALO, lt))
    for sft in range(1, SUBLANES):
        zr_ref[sft, 0:ext - SUBLANES, :] = zr_ref[0, sft:sft + ext - SUBLANES, :]

    prev = None
    for lt in range(nl):
        lanes = slice(lt * LANES, (lt + 1) * LANES)
        wk = [wdw_ref[kk, :, lanes] for kk in range(CONV_WIDTH)]
        for g in range(sr // SUBLANES):
            acc = _zero_after(prev) if chain and prev is not None else None
            for kk in range(CONV_WIDTH):
                a, sft = divmod(CONV_HALO - CONV_PAD + kk, SUBLANES)
                term = zr_ref[sft, (g + a) * SUBLANES:(g + a + 1) * SUBLANES, lanes] * wk[kk]
                acc = term if acc is None else acc + term
            y_ref[g * SUBLANES:(g + 1) * SUBLANES, lanes] = acc
            prev = acc

    bdw = bdw_ref[...]
    lg = lg_ref[...]
    lb = lb_ref[...]
    for base in range(0, sr, NORM_CHUNK):
        y = y_ref[base:base + NORM_CHUNK, :] + bdw
        mu = jnp.mean(y, axis=-1, keepdims=True)
        yc = y - mu
        var = jnp.mean(yc * yc, axis=-1, keepdims=True)
        yn = yc * lax.rsqrt(var + EPS) * lg + lb
        zc_ref[base:base + NORM_CHUNK, :] = (yn * jax.nn.sigmoid(yn)).astype(BF16)


def _inproj_conv_kernel(x_ref, g_ref, w_ref, wdw_ref, bdw_ref, lg_ref, lb_ref, o_ref, zc_ref,
                        u_ref, ring_ref, zr_ref, y_ref, *, n_blocks, bps, n_j, q_scale):
    i = pl.program_id(0)
    j = pl.program_id(1)
    ca_j, cb_j = n_j - 3, n_j - 2
    n_sub = n_j - 1
    in_grid = i < n_blocks

    @pl.when(jnp.logical_and(in_grid, j == 0))
    def _():
        x = x_ref[...]
        ms = jnp.mean(x * x, axis=-1, keepdims=True)
        u_ref[...] = (x * lax.rsqrt(ms + EPS) * g_ref[...]).astype(BF16)

    @pl.when(jnp.logical_and(i == 0, j == 0))
    def _():
        ring_ref[2:4] = jnp.zeros((2,) + ring_ref.shape[1:], BF16)

    def project():
        acc = jnp.dot(u_ref[...], w_ref[...], preferred_element_type=F32)
        acc = acc * jnp.where(j == 0, q_scale, 1.0).astype(F32)
        for s in range(o_ref.shape[0]):
            o_ref[s] = acc[:, s * LANES:(s + 1) * LANES].astype(BF16)

    def conv(chain):
        s = jnp.where(j == n_j - 1, n_sub - 1, jnp.minimum(j, n_sub - 2))
        _conv_sub_block(ring_ref, wdw_ref, bdw_ref, lg_ref, lb_ref, zc_ref, zr_ref, y_ref,
                        c=i - 1, s=s, n_sub=n_sub, bps=bps, chain=chain)

    @pl.when(jnp.logical_and(in_grid, jnp.logical_or(i == 0, j == cb_j)))
    def _():
        project()

    @pl.when(jnp.logical_and(jnp.logical_and(i >= 1, in_grid), j != cb_j))
    def _():
        conv(chain=True)
        project()

    @pl.when(jnp.logical_and(i == n_blocks, j != cb_j))
    def _():
        conv(chain=False)

    @pl.when(jnp.logical_and(in_grid, jnp.logical_or(j == ca_j, j == cb_j)))
    def _():
        ring_ref[2 * (i % 2) + (j - ca_j)] = o_ref[...]


def _inproj_conv(x2, g, w, w_dw, b_dw, ln_g, ln_b, s, d_attn, d_conv):
    t, d = x2.shape
    n = w.shape[1]
    tm = _pick(s, 1024)
    tn = d_attn
    assert d_conv == tn and tn % LANES == 0 and n % tn == 0 and t % tm == 0
    nl = tn // LANES
    n_j = n // tn
    n_sub = n_j - 1
    ca_tile = 3 * d_attn // tn
    assert tm % n_sub == 0
    sr = tm // n_sub
    assert sr % NORM_CHUNK == 0 and sr % CONV_HALO == 0
    n_blocks = t // tm
    rest = [c for c in range(n_j) if c not in (ca_tile, ca_tile + 1)]
    order = rest[:-1] + [ca_tile, ca_tile + 1] + rest[-1:]
    assert order[0] == 0

    def tile(i, j):
        return jnp.where(i == n_blocks, order[-1], _select(j, order))

    def blk(i):
        return jnp.minimum(i, n_blocks - 1)

    def zc_index(i, j):
        sub = jnp.where(j == n_j - 1, n_sub - 1, jnp.minimum(j, n_sub - 2))
        return jnp.where(i == 0, 0, (i - 1) * n_sub + sub)

    def vec():
        return pl.BlockSpec((1, d_conv), lambda i, j: (0, 0))

    w_b = jnp.broadcast_to(w_dw[:, None, :], (CONV_WIDTH, SUBLANES, d_conv))
    kern = functools.partial(_inproj_conv_kernel, n_blocks=n_blocks, bps=s // tm, n_j=n_j,
                             q_scale=HEAD_DIM ** -0.5 * LOG2E)
    return pl.pallas_call(
        kern,
        out_shape=(jax.ShapeDtypeStruct((n // LANES, t, LANES), BF16), jax.ShapeDtypeStruct((t, d_conv), BF16)),
        grid=(n_blocks + 1, n_j),
        in_specs=[
            pl.BlockSpec((tm, d), lambda i, j: (blk(i), 0)),
            pl.BlockSpec((1, d), lambda i, j: (0, 0)),
            pl.BlockSpec((d, tn), lambda i, j: (0, tile(i, j))),
            pl.BlockSpec((CONV_WIDTH, SUBLANES, d_conv), lambda i, j: (0, 0, 0)),
            vec(), vec(), vec(),
        ],
        out_specs=(pl.BlockSpec((nl, tm, LANES), lambda i, j: (tile(i, j), blk(i), 0)),
                   pl.BlockSpec((sr, d_conv), lambda i, j: (zc_index(i, j), 0))),
        scratch_shapes=[pltpu.VMEM((tm, d), BF16),
                        pltpu.VMEM((4, nl, tm, LANES), BF16),
                        pltpu.VMEM((SUBLANES, sr + 2 * CONV_HALO, d_conv), F32),
                        pltpu.VMEM((sr, d_conv), F32)],
        compiler_params=_params("arbitrary", "arbitrary"),
        name="inproj_conv",
    )(x2, g.reshape(1, d), w, w_b, b_dw.reshape(1, -1), ln_g.reshape(1, -1), ln_b.reshape(1, -1))


def _attn_block_types(rows, r):
    kwr = r + WIN_R - 1
    kwr += (kwr * GRID_W) % LANES // GRID_W
    kwr = min(kwr, rows)
    nb = rows // r
    interior = [i for i in range(nb) if WIN_R // 2 <= i * r <= min(rows - kwr + WIN_R // 2,
                                                                  rows - r - (WIN_R // 2 - 1))]
    assert interior and interior == list(range(interior[0], interior[-1] + 1))
    n_top = interior[0]
    n_bot = nb - 1 - interior[-1]
    type_r0 = [interior[0] * r] + [i * r for i in range(n_top)] + [i * r for i in range(nb - n_bot, nb)]
    return kwr, nb, n_top, n_bot, type_r0


def _attn_bias(rpb, rows, r):
    kwr, _, _, _, type_r0 = _attn_block_types(rows, r)
    kr_sz = min(WIN_R, rows)
    kc_sz = min(WIN_C, GRID_W)
    heads, n_dr, _ = rpb.shape
    rpb = rpb.astype(F32) * LOG2E
    pad = jnp.full((heads, n_dr, GRID_W), MASK_BIAS, F32)
    padded = jnp.concatenate([pad, rpb, pad], axis=-1)
    colb = jnp.stack([padded[:, :, GRID_W + WIN_C - 1 - c:2 * GRID_W + WIN_C - 1 - c] for c in range(GRID_W)],
                     axis=2)
    col = np.arange(GRID_W)
    cs = np.clip(col - kc_sz // 2, 0, GRID_W - kc_sz)
    col_ok = (col[None, :] >= cs[:, None]) & (col[None, :] < cs[:, None] + kc_sz)
    colb = jnp.where(col_ok[None, None], colb, MASK_BIAS)
    masked = jnp.full((heads, GRID_W, GRID_W), MASK_BIAS, F32)
    types = []
    for r0 in type_r0:
        ws = int(np.clip(r0 - kr_sz // 2, 0, rows - kwr))
        q_rows = []
        for a in range(r):
            qr = r0 + a
            rs = int(np.clip(qr - kr_sz // 2, 0, rows - kr_sz))
            blocks = []
            for wi in range(kwr):
                kr = ws + wi
                blocks.append(colb[:, kr - qr + WIN_R - 1] if rs <= kr < rs + kr_sz else masked)
            q_rows.append(jnp.concatenate(blocks, axis=2))
        types.append(jnp.concatenate(q_rows, axis=1))
    return jnp.stack(types, axis=1)


def _attn_kernel(q_ref, k_ref, v_ref, b_ref, o_ref, *, rows, r, kwr, sub, nb, n_top, n_bot):
    step = pl.program_id(2)
    rq = r * GRID_W
    kw = kwr * GRID_W
    for sb in range(sub):
        i = step * sub + sb
        ws = jnp.clip(i * r - WIN_R // 2, 0, rows - kwr)
        bt = jnp.where(i < n_top, 1 + i, jnp.where(i >= nb - n_bot, 1 + n_top + i - (nb - n_bot), 0))
        start = pl.multiple_of(ws * GRID_W, GRID_W)
        q = q_ref[0, sb * rq:(sb + 1) * rq, :]
        k = k_ref[0, pl.ds(start, kw), :]
        v = v_ref[0, pl.ds(start, kw), :]
        s = lax.dot_general(q, k, (((1,), (1,)), ((), ())), preferred_element_type=F32)
        s = s + b_ref[0, bt]
        m = jnp.max(s, axis=-1, keepdims=True)
        p = jnp.exp2(s - m)
        l = jnp.sum(p, axis=-1, keepdims=True)
        o = jnp.dot(p.astype(BF16), v, preferred_element_type=F32)
        o_ref[0, sb * rq:(sb + 1) * rq, :] = (o / l).astype(BF16)


def _attention(proj_s, rpb, b, s, d_attn):
    heads = d_attn // HEAD_DIM
    rows = s // GRID_W
    r = ATTN_ROWS if rows % ATTN_ROWS == 0 and rows >= 2 * WIN_R else 1
    kwr, nb, n_top, n_bot, type_r0 = _attn_block_types(rows, r)
    sub = _pick(nb, ATTN_SUB)
    bias = _attn_bias(rpb, rows, r)
    rq, kw = r * GRID_W, kwr * GRID_W
    steps = nb // sub
    kern = functools.partial(_attn_kernel, rows=rows, r=r, kwr=kwr, sub=sub, nb=nb, n_top=n_top, n_bot=n_bot)
    return pl.pallas_call(
        kern,
        out_shape=jax.ShapeDtypeStruct((b, s, d_attn), BF16),
        grid=(b, heads, steps),
        in_specs=[
            pl.BlockSpec((1, rq * sub, HEAD_DIM), lambda bi, h, i: (h, bi * steps + i, 0)),
            pl.BlockSpec((1, s, HEAD_DIM), lambda bi, h, i: (heads + h, bi, 0)),
            pl.BlockSpec((1, s, HEAD_DIM), lambda bi, h, i: (2 * heads + h, bi, 0)),
            pl.BlockSpec((1, len(type_r0), rq, kw), lambda bi, h, i: (h, 0, 0, 0)),
        ],
        out_specs=pl.BlockSpec((1, rq * sub, HEAD_DIM), lambda bi, h, i: (bi, i, h)),
        compiler_params=_params("parallel", "parallel", "arbitrary"),
        name="nattn",
    )(proj_s, proj_s, proj_s, bias)


def _merge_kernel(*refs, n_gate):
    at_ref, zc_ref = refs[:2]
    ga_refs = refs[2:2 + n_gate]
    gb_refs = refs[2 + n_gate:2 + 2 * n_gate]
    x_ref, wa_ref, wc_ref, wo_ref, g_ref, gn_ref, o_ref, u_ref = refs[2 + 2 * n_gate:]
    ns = ga_refs[0].shape[0]
    ya = jnp.dot(at_ref[...], wa_ref[...], preferred_element_type=F32)
    yc = jnp.dot(zc_ref[...], wc_ref[...], preferred_element_type=F32)
    parts = []
    for p in range(n_gate):
        for s in range(ns):
            c0 = (p * ns + s) * LANES
            ga = jax.nn.sigmoid(ga_refs[p][s].astype(F32))
            gb = jax.nn.sigmoid(gb_refs[p][s].astype(F32))
            parts.append((ga * ya[:, c0:c0 + LANES] + gb * yc[:, c0:c0 + LANES]).astype(BF16))
    merged = jnp.concatenate(parts, axis=-1)
    o = jnp.dot(merged, wo_ref[...], preferred_element_type=F32)
    ms = jnp.mean(o * o, axis=-1, keepdims=True)
    h = x_ref[...] + o * lax.rsqrt(ms + EPS) * g_ref[...]
    o_ref[...] = h
    hs = jnp.mean(h * h, axis=-1, keepdims=True)
    u_ref[...] = (h * lax.rsqrt(hs + EPS) * gn_ref[...]).astype(BF16)


def _merge(attn2, zc2, proj_s, x2, w_attn_o, w_conv_o, w_out, g, g_next, slab_ga):
    t, d = x2.shape
    ns = math.gcd(slab_ga, d // LANES)
    n_gate = d // LANES // ns
    blk = slab_ga // ns
    da = attn2.shape[1]
    dc = zc2.shape[1]
    tm = _pick(t, 256)

    def wspec(shape):
        return pl.BlockSpec(shape, lambda i: (0, 0), pipeline_mode=pl.Buffered(1))

    def gspec(p):
        return pl.BlockSpec((ns, tm, LANES), lambda i: (blk + p, i, 0))

    return pl.pallas_call(
        functools.partial(_merge_kernel, n_gate=n_gate),
        out_shape=(jax.ShapeDtypeStruct((t, d), F32), jax.ShapeDtypeStruct((t, d), BF16)),
        grid=(t // tm,),
        in_specs=[
            pl.BlockSpec((tm, da), lambda i: (i, 0)),
            pl.BlockSpec((tm, dc), lambda i: (i, 0)),
            *[gspec(p) for p in range(2 * n_gate)],
            pl.BlockSpec((tm, d), lambda i: (i, 0)),
            wspec((da, d)), wspec((dc, d)), wspec((d, d)),
            pl.BlockSpec((1, d), lambda i: (0, 0)),
            pl.BlockSpec((1, d), lambda i: (0, 0)),
        ],
        out_specs=(pl.BlockSpec((tm, d), lambda i: (i, 0)), pl.BlockSpec((tm, d), lambda i: (i, 0))),
        compiler_params=_params("parallel"),
        name="merge",
    )(attn2, zc2, *([proj_s] * (2 * n_gate)), x2, w_attn_o, w_conv_o, w_out, g.reshape(1, d),
      g_next.reshape(1, d))


def _ffn_up_kernel(u_ref, wg_ref, wu_ref, o_ref):
    u = u_ref[...]
    gate = jnp.dot(u, wg_ref[...], preferred_element_type=F32)
    up = jnp.dot(u, wu_ref[...], preferred_element_type=F32)
    o_ref[...] = (gate * jax.nn.sigmoid(gate) * up).astype(BF16)


def _ffn_up(u2, w_gate_up):
    t, d = u2.shape
    dff = w_gate_up.shape[1] // 2
    tm = _pick(t, 1024)
    tf = _pick(dff, 512)
    nf = dff // tf
    return pl.pallas_call(
        _ffn_up_kernel,
        out_shape=jax.ShapeDtypeStruct((t, dff), BF16),
        grid=(t // tm, nf),
        in_specs=[
            pl.BlockSpec((tm, d), lambda i, j: (i, 0)),
            pl.BlockSpec((d, tf), lambda i, j: (0, j)),
            pl.BlockSpec((d, tf), lambda i, j: (0, nf + j)),
        ],
        out_specs=pl.BlockSpec((tm, tf), lambda i, j: (i, j)),
        compiler_params=_params("parallel", "arbitrary"),
        name="ffn_up",
    )(u2, w_gate_up, w_gate_up)


def _ffn_down_kernel(a_ref, w_ref, h_ref, g_ref, o_ref):
    k = pl.program_id(1)
    last = pl.num_programs(1) - 1

    def partial_sum():
        return jnp.dot(a_ref[...], w_ref[...], preferred_element_type=F32)

    @pl.when(k == 0)
    def _():
        o_ref[...] = partial_sum()

    @pl.when(jnp.logical_and(k > 0, k < last))
    def _():
        o_ref[...] += partial_sum()

    @pl.when(k == last)
    def _():
        f = o_ref[...] + partial_sum()
        ms = jnp.mean(f * f, axis=-1, keepdims=True)
        o_ref[...] = h_ref[...] + f * lax.rsqrt(ms + EPS) * g_ref[...]


def _ffn_down(act, w_down, h2, g):
    t, d = h2.shape
    dff = act.shape[1]
    tm = _pick(t, 1024)
    tk = _pick(dff, 512)
    assert dff // tk >= 2
    return pl.pallas_call(
        _ffn_down_kernel,
        out_shape=jax.ShapeDtypeStruct((t, d), F32),
        grid=(t // tm, dff // tk),
        in_specs=[
            pl.BlockSpec((tm, tk), lambda i, k: (i, k)),
            pl.BlockSpec((tk, d), lambda i, k: (k, 0)),
            pl.BlockSpec((tm, d), lambda i, k: (i, 0)),
            pl.BlockSpec((1, d), lambda i, k: (0, 0)),
        ],
        out_specs=pl.BlockSpec((tm, d), lambda i, k: (i, 0)),
        compiler_params=_params("parallel", "arbitrary"),
        name="ffn_down",
    )(act, w_down, h2, g.reshape(1, d))


def kernel(x, mix_pre_g, mix_post_g, w_in, rpb, w_attn_o, w_dw, b_dw, conv_ln_g, conv_ln_b, w_conv_o, w_out,
           ffn_pre_g, ffn_post_g, w_gate_up, w_down):
    b, s, d = x.shape
    t = b * s
    depth = w_in.shape[0]
    d_attn = w_attn_o.shape[1]
    d_conv = w_conv_o.shape[1]
    assert d_attn == d_conv and w_in.shape[2] == 3 * d_attn + 2 * d_conv + 2 * d and d == 2 * d_conv
    h = x.reshape(t, d)
    for l in range(depth):
        proj_s, zc = _inproj_conv(h, mix_pre_g[l], w_in[l].astype(BF16), w_dw[l], b_dw[l], conv_ln_g[l],
                                  conv_ln_b[l], s, d_attn, d_conv)
        attn = _attention(proj_s, rpb[l], b, s, d_attn)
        h, u2 = _merge(attn.reshape(t, d_attn), zc.reshape(t, d_conv), proj_s, h, w_attn_o[l].astype(BF16),
                       w_conv_o[l].astype(BF16), w_out[l].astype(BF16), mix_post_g[l], ffn_pre_g[l],
                       (3 * d_attn + 2 * d_conv) // LANES)
        act = _ffn_up(u2, w_gate_up[l].astype(BF16))
        h = _ffn_down(act, w_down[l].astype(BF16), h, ffn_post_g[l])
    return h.reshape(b, s, d)
```

```python
import functools
import math

import jax
import jax.numpy as jnp
import numpy as np
from jax import lax
from jax.experimental import pallas as pl
from jax.experimental.pallas import tpu as pltpu

GRID_W = 64
WIN_R = 8
WIN_C = 16
HEAD_DIM = 128
CONV_WIDTH = 31
CONV_PAD = CONV_WIDTH // 2
EPS = 1e-6
MASK_BIAS = -1e30
LOG2E = math.log2(math.e)

LANES = 128
SUBLANES = 8
VMEM_LIMIT_BYTES = 56 * 1024 * 1024

ATTN_ROWS = 4
ATTN_SUB = 8
FFN_DOWN_ROWS = 256
CONV_HALO = 16
NORM_CHUNK = 16
CONV_CHAINS = 2

BF16 = jnp.bfloat16
F32 = jnp.float32


def _params(*sem, flags=None):
    return pltpu.CompilerParams(dimension_semantics=sem, vmem_limit_bytes=VMEM_LIMIT_BYTES, flags=flags)


def _pick(n, pref):
    t = min(pref, n)
    while n % t:
        t //= 2
    return t


def _zero_after(v):
    bits = lax.bitcast_convert_type(v, jnp.uint32)
    return lax.bitcast_convert_type((bits >> 16) >> 16, F32)


def _select(j, values):
    out = values[-1]
    for idx in range(len(values) - 2, -1, -1):
        out = jnp.where(j == idx, values[idx], out)
    return out


def _conv_sub_block(ring_ref, wdw_ref, bdw_ref, lg_ref, lb_ref, zc_ref, zr_ref, y_ref, *, c, s, n_sub, bps,
                    chain):
    nl, tm = ring_ref.shape[1], ring_ref.shape[2]
    sr = tm // n_sub
    ext = sr + 2 * CONV_HALO
    slot = c % 2
    first = s == 0
    last = s == n_sub - 1
    row0 = pl.multiple_of(s * sr, sr)
    pslot = jnp.where(first, 1 - slot, slot)
    prow = pl.multiple_of(jnp.where(first, tm - CONV_HALO, row0 - CONV_HALO), CONV_HALO)
    nslot = jnp.where(last, 1 - slot, slot)
    nrow = pl.multiple_of(jnp.where(last, 0, row0 + sr), CONV_HALO)
    pzero = jnp.logical_and(first, c % bps == 0)
    nzero = jnp.logical_and(last, c % bps == bps - 1)

    def glu(sl, row, nrows, lt):
        a = ring_ref[2 * sl, lt, pl.ds(row, nrows), :].astype(F32)
        b = ring_ref[2 * sl + 1, lt, pl.ds(row, nrows), :].astype(F32)
        return a * jax.nn.sigmoid(b)

    for lt in range(nl):
        lanes = slice(lt * LANES, (lt + 1) * LANES)
        zr_ref[0, 0:CONV_HALO, lanes] = jnp.where(pzero, 0.0, glu(pslot, prow, CONV_HALO, lt))
        zr_ref[0, CONV_HALO:CONV_HALO + sr, lanes] = glu(slot, row0, sr, lt)
        zr_ref[0, CONV_HALO + sr:ext, lanes] = jnp.where(nzero, 0.0, glu(nslot, nrow, CONV_HALO, lt))
    for sft in range(1, SUBLANES):
        zr_ref[sft, 0:ext - SUBLANES, :] = zr_ref[0, sft:sft + ext - SUBLANES, :]

    prev = [None] * CONV_CHAINS
    for lt in range(nl):
        lanes = slice(lt * LANES, (lt + 1) * LANES)
        wk = [wdw_ref[kk, :, lanes] for kk in range(CONV_WIDTH)]
        for g in range(sr // SUBLANES):
            ch = g % CONV_CHAINS
            acc = _zero_after(prev[ch]) if chain and prev[ch] is not None else None
            for kk in range(CONV_WIDTH):
                a, sft = divmod(CONV_HALO - CONV_PAD + kk, SUBLANES)
                term = zr_ref[sft, (g + a) * SUBLANES:(g + a + 1) * SUBLANES, lanes] * wk[kk]
                acc = term if acc is None else acc + term
            y_ref[g * SUBLANES:(g + 1) * SUBLANES, lanes] = acc
            prev[ch] = acc

    bdw = bdw_ref[...]
    lg = lg_ref[...]
    lb = lb_ref[...]
    for base in range(0, sr, NORM_CHUNK):
        y = y_ref[base:base + NORM_CHUNK, :] + bdw
        mu = jnp.mean(y, axis=-1, keepdims=True)
        yc = y - mu
        var = jnp.mean(yc * yc, axis=-1, keepdims=True)
        yn = yc * lax.rsqrt(var + EPS) * lg + lb
        zc_ref[base:base + NORM_CHUNK, :] = (yn * jax.nn.sigmoid(yn)).astype(BF16)


def _inproj_conv_kernel(x_ref, g_ref, w_ref, wdw_ref, bdw_ref, lg_ref, lb_ref, o_ref, zc_ref,
                        u_ref, ring_ref, zr_ref, y_ref, *, n_blocks, bps, n_j, q_scale):
    i = pl.program_id(0)
    j = pl.program_id(1)
    ca_j, cb_j = n_j - 3, n_j - 2
    n_sub = n_j - 1
    in_grid = i < n_blocks

    @pl.when(jnp.logical_and(in_grid, j == 0))
    def _():
        x = x_ref[...]
        ms = jnp.mean(x * x, axis=-1, keepdims=True)
        u_ref[...] = (x * lax.rsqrt(ms + EPS) * g_ref[...]).astype(BF16)

    @pl.when(jnp.logical_and(i == 0, j == 0))
    def _():
        ring_ref[2:4] = jnp.zeros((2,) + ring_ref.shape[1:], BF16)

    def project():
        acc = jnp.dot(u_ref[...], w_ref[...], preferred_element_type=F32)
        acc = acc * jnp.where(j == 0, q_scale, 1.0).astype(F32)
        for s in range(o_ref.shape[0]):
            o_ref[s] = acc[:, s * LANES:(s + 1) * LANES].astype(BF16)

    def conv(chain):
        s = jnp.where(j == n_j - 1, n_sub - 1, jnp.minimum(j, n_sub - 2))
        _conv_sub_block(ring_ref, wdw_ref, bdw_ref, lg_ref, lb_ref, zc_ref, zr_ref, y_ref,
                        c=i - 1, s=s, n_sub=n_sub, bps=bps, chain=chain)

    @pl.when(jnp.logical_and(in_grid, jnp.logical_or(i == 0, j == cb_j)))
    def _():
        project()

    @pl.when(jnp.logical_and(jnp.logical_and(i >= 1, in_grid), j != cb_j))
    def _():
        conv(chain=True)
        project()

    @pl.when(jnp.logical_and(i == n_blocks, j != cb_j))
    def _():
        conv(chain=False)

    @pl.when(jnp.logical_and(in_grid, jnp.logical_or(j == ca_j, j == cb_j)))
    def _():
        ring_ref[2 * (i % 2) + (j - ca_j)] = o_ref[...]


def _inproj_conv(x2, g, w, w_dw, b_dw, ln_g, ln_b, s, d_attn, d_conv):
    t, d = x2.shape
    n = w.shape[1]
    tm = _pick(s, 1024)
    tn = d_attn
    assert d_conv == tn and tn % LANES == 0 and n % tn == 0 and t % tm == 0
    nl = tn // LANES
    n_j = n // tn
    n_sub = n_j - 1
    ca_tile = 3 * d_attn // tn
    assert tm % n_sub == 0
    sr = tm // n_sub
    assert sr % NORM_CHUNK == 0 and sr % CONV_HALO == 0
    n_blocks = t // tm
    rest = [c for c in range(n_j) if c not in (ca_tile, ca_tile + 1)]
    order = rest[:-1] + [ca_tile, ca_tile + 1] + rest[-1:]
    assert order[0] == 0

    def tile(i, j):
        return jnp.where(i == n_blocks, order[-1], _select(j, order))

    def blk(i):
        return jnp.minimum(i, n_blocks - 1)

    def zc_index(i, j):
        sub = jnp.where(j == n_j - 1, n_sub - 1, jnp.minimum(j, n_sub - 2))
        return jnp.where(i == 0, 0, (i - 1) * n_sub + sub)

    def vec():
        return pl.BlockSpec((1, d_conv), lambda i, j: (0, 0))

    w_b = jnp.broadcast_to(w_dw[:, None, :], (CONV_WIDTH, SUBLANES, d_conv))
    kern = functools.partial(_inproj_conv_kernel, n_blocks=n_blocks, bps=s // tm, n_j=n_j,
                             q_scale=HEAD_DIM ** -0.5 * LOG2E)
    return pl.pallas_call(
        kern,
        out_shape=(jax.ShapeDtypeStruct((n // LANES, t, LANES), BF16), jax.ShapeDtypeStruct((t, d_conv), BF16)),
        grid=(n_blocks + 1, n_j),
        in_specs=[
            pl.BlockSpec((tm, d), lambda i, j: (blk(i), 0)),
            pl.BlockSpec((1, d), lambda i, j: (0, 0)),
            pl.BlockSpec((d, tn), lambda i, j: (0, tile(i, j))),
            pl.BlockSpec((CONV_WIDTH, SUBLANES, d_conv), lambda i, j: (0, 0, 0)),
            vec(), vec(), vec(),
        ],
        out_specs=(pl.BlockSpec((nl, tm, LANES), lambda i, j: (tile(i, j), blk(i), 0)),
                   pl.BlockSpec((sr, d_conv), lambda i, j: (zc_index(i, j), 0))),
        scratch_shapes=[pltpu.VMEM((tm, d), BF16),
                        pltpu.VMEM((4, nl, tm, LANES), BF16),
                        pltpu.VMEM((SUBLANES, sr + 2 * CONV_HALO, d_conv), F32),
                        pltpu.VMEM((sr, d_conv), F32)],
        compiler_params=_params("arbitrary", "arbitrary"),
        name="inproj_conv",
    )(x2, g.reshape(1, d), w, w_b, b_dw.reshape(1, -1), ln_g.reshape(1, -1), ln_b.reshape(1, -1))


def _attn_block_types(rows, r):
    kwr = r + WIN_R - 1
    kwr += (kwr * GRID_W) % LANES // GRID_W
    kwr = min(kwr, rows)
    nb = rows // r
    interior = [i for i in range(nb) if WIN_R // 2 <= i * r <= min(rows - kwr + WIN_R // 2,
                                                                  rows - r - (WIN_R // 2 - 1))]
    assert interior and interior == list(range(interior[0], interior[-1] + 1))
    n_top = interior[0]
    n_bot = nb - 1 - interior[-1]
    type_r0 = [interior[0] * r] + [i * r for i in range(n_top)] + [i * r for i in range(nb - n_bot, nb)]
    return kwr, nb, n_top, n_bot, type_r0


def _attn_bias(rpb, rows, r):
    kwr, _, _, _, type_r0 = _attn_block_types(rows, r)
    kr_sz = min(WIN_R, rows)
    kc_sz = min(WIN_C, GRID_W)
    heads, n_dr, _ = rpb.shape
    rpb = rpb.astype(F32) * LOG2E
    pad = jnp.full((heads, n_dr, GRID_W), MASK_BIAS, F32)
    padded = jnp.concatenate([pad, rpb, pad], axis=-1)
    colb = jnp.stack([padded[:, :, GRID_W + WIN_C - 1 - c:2 * GRID_W + WIN_C - 1 - c] for c in range(GRID_W)],
                     axis=2)
    col = np.arange(GRID_W)
    cs = np.clip(col - kc_sz // 2, 0, GRID_W - kc_sz)
    col_ok = (col[None, :] >= cs[:, None]) & (col[None, :] < cs[:, None] + kc_sz)
    colb = jnp.where(col_ok[None, None], colb, MASK_BIAS)
    masked = jnp.full((heads, GRID_W, GRID_W), MASK_BIAS, F32)
    types = []
    for r0 in type_r0:
        ws = int(np.clip(r0 - kr_sz // 2, 0, rows - kwr))
        q_rows = []
        for a in range(r):
            qr = r0 + a
            rs = int(np.clip(qr - kr_sz // 2, 0, rows - kr_sz))
            blocks = []
            for wi in range(kwr):
                kr = ws + wi
                blocks.append(colb[:, kr - qr + WIN_R - 1] if rs <= kr < rs + kr_sz else masked)
            q_rows.append(jnp.concatenate(blocks, axis=2))
        types.append(jnp.concatenate(q_rows, axis=1))
    return jnp.stack(types, axis=1)


def _attn_kernel(q_ref, k_ref, v_ref, b_ref, o_ref, *, rows, r, kwr, sub, nb, n_top, n_bot):
    step = pl.program_id(2)
    rq = r * GRID_W
    kw = kwr * GRID_W
    for sb in range(sub):
        i = step * sub + sb
        ws = jnp.clip(i * r - WIN_R // 2, 0, rows - kwr)
        bt = jnp.where(i < n_top, 1 + i, jnp.where(i >= nb - n_bot, 1 + n_top + i - (nb - n_bot), 0))
        start = pl.multiple_of(ws * GRID_W, GRID_W)
        q = q_ref[0, sb * rq:(sb + 1) * rq, :]
        k = k_ref[0, pl.ds(start, kw), :]
        v = v_ref[0, pl.ds(start, kw), :]
        s = lax.dot_general(q, k, (((1,), (1,)), ((), ())), preferred_element_type=F32)
        s = s + b_ref[0, bt]
        m = jnp.max(s, axis=-1, keepdims=True)
        p = jnp.exp2(s - m)
        l = jnp.sum(p, axis=-1, keepdims=True)
        o = jnp.dot(p.astype(BF16), v, preferred_element_type=F32)
        o_ref[0, sb * rq:(sb + 1) * rq, :] = (o / l).astype(BF16)


def _attention(proj_s, rpb, b, s, d_attn):
    heads = d_attn // HEAD_DIM
    rows = s // GRID_W
    r = ATTN_ROWS if rows % ATTN_ROWS == 0 and rows >= 2 * WIN_R else 1
    kwr, nb, n_top, n_bot, type_r0 = _attn_block_types(rows, r)
    sub = _pick(nb, ATTN_SUB)
    bias = _attn_bias(rpb, rows, r)
    rq, kw = r * GRID_W, kwr * GRID_W
    steps = nb // sub
    kern = functools.partial(_attn_kernel, rows=rows, r=r, kwr=kwr, sub=sub, nb=nb, n_top=n_top, n_bot=n_bot)
    return pl.pallas_call(
        kern,
        out_shape=jax.ShapeDtypeStruct((b, s, d_attn), BF16),
        grid=(b, heads, steps),
        in_specs=[
            pl.BlockSpec((1, rq * sub, HEAD_DIM), lambda bi, h, i: (h, bi * steps + i, 0)),
            pl.BlockSpec((1, s, HEAD_DIM), lambda bi, h, i: (heads + h, bi, 0)),
            pl.BlockSpec((1, s, HEAD_DIM), lambda bi, h, i: (2 * heads + h, bi, 0)),
            pl.BlockSpec((1, len(type_r0), rq, kw), lambda bi, h, i: (h, 0, 0, 0)),
        ],
        out_specs=pl.BlockSpec((1, rq * sub, HEAD_DIM), lambda bi, h, i: (bi, i, h)),
        compiler_params=_params("parallel", "parallel", "arbitrary"),
        name="nattn",
    )(proj_s, proj_s, proj_s, bias)


def _merge_kernel(*refs, n_gate):
    at_ref, zc_ref = refs[:2]
    ga_refs = refs[2:2 + n_gate]
    gb_refs = refs[2 + n_gate:2 + 2 * n_gate]
    x_ref, wa_ref, wc_ref, wo_ref, g_ref, gn_ref, o_ref, u_ref = refs[2 + 2 * n_gate:]
    ns = ga_refs[0].shape[0]
    ya = jnp.dot(at_ref[...], wa_ref[...], preferred_element_type=F32)
    yc = jnp.dot(zc_ref[...], wc_ref[...], preferred_element_type=F32)
    parts = []
    for p in range(n_gate):
        for s in range(ns):
            c0 = (p * ns + s) * LANES
            ga = jax.nn.sigmoid(ga_refs[p][s].astype(F32))
            gb = jax.nn.sigmoid(gb_refs[p][s].astype(F32))
            parts.append((ga * ya[:, c0:c0 + LANES] + gb * yc[:, c0:c0 + LANES]).astype(BF16))
    merged = jnp.concatenate(parts, axis=-1)
    o = jnp.dot(merged, wo_ref[...], preferred_element_type=F32)
    ms = jnp.mean(o * o, axis=-1, keepdims=True)
    h = x_ref[...] + o * lax.rsqrt(ms + EPS) * g_ref[...]
    o_ref[...] = h
    hs = jnp.mean(h * h, axis=-1, keepdims=True)
    u_ref[...] = (h * lax.rsqrt(hs + EPS) * gn_ref[...]).astype(BF16)


def _merge(attn2, zc2, proj_s, x2, w_attn_o, w_conv_o, w_out, g, g_next, slab_ga):
    t, d = x2.shape
    ns = math.gcd(slab_ga, d // LANES)
    n_gate = d // LANES // ns
    blk = slab_ga // ns
    da = attn2.shape[1]
    dc = zc2.shape[1]
    tm = _pick(t, 256)

    def wspec(shape):
        return pl.BlockSpec(shape, lambda i: (0, 0), pipeline_mode=pl.Buffered(1))

    def gspec(p):
        return pl.BlockSpec((ns, tm, LANES), lambda i: (blk + p, i, 0))

    return pl.pallas_call(
        functools.partial(_merge_kernel, n_gate=n_gate),
        out_shape=(jax.ShapeDtypeStruct((t, d), F32), jax.ShapeDtypeStruct((t, d), BF16)),
        grid=(t // tm,),
        in_specs=[
            pl.BlockSpec((tm, da), lambda i: (i, 0)),
            pl.BlockSpec((tm, dc), lambda i: (i, 0)),
            *[gspec(p) for p in range(2 * n_gate)],
            pl.BlockSpec((tm, d), lambda i: (i, 0)),
            wspec((da, d)), wspec((dc, d)), wspec((d, d)),
            pl.BlockSpec((1, d), lambda i: (0, 0)),
            pl.BlockSpec((1, d), lambda i: (0, 0)),
        ],
        out_specs=(pl.BlockSpec((tm, d), lambda i: (i, 0)), pl.BlockSpec((tm, d), lambda i: (i, 0))),
        compiler_params=_params("parallel"),
        name="merge",
    )(attn2, zc2, *([proj_s] * (2 * n_gate)), x2, w_attn_o, w_conv_o, w_out, g.reshape(1, d),
      g_next.reshape(1, d))


def _ffn_up_kernel(u_ref, wg_ref, wu_ref, o_ref):
    u = u_ref[...]
    gate = jnp.dot(u, wg_ref[...], preferred_element_type=F32)
    up = jnp.dot(u, wu_ref[...], preferred_element_type=F32)
    o_ref[...] = (gate * jax.nn.sigmoid(gate) * up).astype(BF16)


def _ffn_up(u2, w_gate_up):
    t, d = u2.shape
    dff = w_gate_up.shape[1] // 2
    tm = _pick(t, 1024)
    tf = _pick(dff, 512)
    nf = dff // tf
    return pl.pallas_call(
        _ffn_up_kernel,
        out_shape=jax.ShapeDtypeStruct((t, dff), BF16),
        grid=(t // tm, nf),
        in_specs=[
            pl.BlockSpec((tm, d), lambda i, j: (i, 0)),
            pl.BlockSpec((d, tf), lambda i, j: (0, j)),
            pl.BlockSpec((d, tf), lambda i, j: (0, nf + j)),
        ],
        out_specs=pl.BlockSpec((tm, tf), lambda i, j: (i, j)),
        compiler_params=_params("parallel", "arbitrary"),
        name="ffn_up",
    )(u2, w_gate_up, w_gate_up)


def _ffn_down_kernel(a_ref, w_ref, h_ref, g_ref, o_ref):
    f = jnp.dot(a_ref[...], w_ref[...], preferred_element_type=F32)
    ms = jnp.mean(f * f, axis=-1, keepdims=True)
    o_ref[...] = h_ref[...] + f * lax.rsqrt(ms + EPS) * g_ref[...]


def _ffn_down(act, w_down, h2, g):
    t, d = h2.shape
    dff = act.shape[1]
    tm = _pick(t, FFN_DOWN_ROWS)
    return pl.pallas_call(
        _ffn_down_kernel,
        out_shape=jax.ShapeDtypeStruct((t, d), F32),
        grid=(t // tm,),
        in_specs=[
            pl.BlockSpec((tm, dff), lambda i: (i, 0)),
            pl.BlockSpec((dff, d), lambda i: (0, 0), pipeline_mode=pl.Buffered(1)),
            pl.BlockSpec((tm, d), lambda i: (i, 0)),
            pl.BlockSpec((1, d), lambda i: (0, 0)),
        ],
        out_specs=pl.BlockSpec((tm, d), lambda i: (i, 0)),
        compiler_params=_params("parallel"),
        name="ffn_down",
    )(act, w_down, h2, g.reshape(1, d))


def kernel(x, mix_pre_g, mix_post_g, w_in, rpb, w_attn_o, w_dw, b_dw, conv_ln_g, conv_ln_b, w_conv_o, w_out,
           ffn_pre_g, ffn_post_g, w_gate_up, w_down):
    b, s, d = x.shape
    t = b * s
    depth = w_in.shape[0]
    d_attn = w_attn_o.shape[1]
    d_conv = w_conv_o.shape[1]
    assert d_attn == d_conv and w_in.shape[2] == 3 * d_attn + 2 * d_conv + 2 * d and d == 2 * d_conv
    h = x.reshape(t, d)
    for l in range(depth):
        proj_s, zc = _inproj_conv(h, mix_pre_g[l], w_in[l].astype(BF16), w_dw[l], b_dw[l], conv_ln_g[l],
                                  conv_ln_b[l], s, d_attn, d_conv)
        attn = _attention(proj_s, rpb[l], b, s, d_attn)
        h, u2 = _merge(attn.reshape(t, d_attn), zc.reshape(t, d_conv), proj_s, h, w_attn_o[l].astype(BF16),
                       w_conv_o[l].astype(BF16), w_out[l].astype(BF16), mix_post_g[l], ffn_pre_g[l],
                       (3 * d_attn + 2 * d_conv) // LANES)
        act = _ffn_up(u2, w_gate_up[l].astype(BF16))
        h = _ffn_down(act, w_down[l].astype(BF16), h, ffn_post_g[l])
    return h.reshape(b, s, d)
```

```python
import functools
import math

import jax
import jax.numpy as jnp
import numpy as np
from jax import lax
from jax.experimental import pallas as pl
from jax.experimental.pallas import tpu as pltpu

GRID_W = 64
WIN_R = 8
WIN_C = 16
HEAD_DIM = 128
CONV_WIDTH = 31
CONV_PAD = CONV_WIDTH // 2
EPS = 1e-6
MASK_BIAS = -1e30
LOG2E = math.log2(math.e)

LANES = 128
SUBLANES = 8
VMEM_LIMIT_BYTES = 56 * 1024 * 1024

ATTN_ROWS = 4
ATTN_SUB = 8
FFN_DOWN_ROWS = 512
CONV_HALO = 16
NORM_CHUNK = 16
CONV_CHAINS = 2
MERGE_CHUNK = 8

BF16 = jnp.bfloat16
F32 = jnp.float32


def _params(*sem, flags=None):
    return pltpu.CompilerParams(dimension_semantics=sem, vmem_limit_bytes=VMEM_LIMIT_BYTES, flags=flags)


def _pick(n, pref):
    t = min(pref, n)
    while n % t:
        t //= 2
    return t


def _zero_after(v):
    bits = lax.bitcast_convert_type(v, jnp.uint32)
    return lax.bitcast_convert_type((bits >> 16) >> 16, F32)


def _select(j, values):
    out = values[-1]
    for idx in range(len(values) - 2, -1, -1):
        out = jnp.where(j == idx, values[idx], out)
    return out


def _conv_sub_block(ring_ref, wdw_ref, bdw_ref, lg_ref, lb_ref, zc_ref, zr_ref, y_ref, *, c, s, n_sub, bps,
                    chain):
    nl, tm = ring_ref.shape[1], ring_ref.shape[2]
    sr = tm // n_sub
    ext = sr + 2 * CONV_HALO
    slot = c % 2
    first = s == 0
    last = s == n_sub - 1
    row0 = pl.multiple_of(s * sr, sr)
    pslot = jnp.where(first, 1 - slot, slot)
    prow = pl.multiple_of(jnp.where(first, tm - CONV_HALO, row0 - CONV_HALO), CONV_HALO)
    nslot = jnp.where(last, 1 - slot, slot)
    nrow = pl.multiple_of(jnp.where(last, 0, row0 + sr), CONV_HALO)
    pzero = jnp.logical_and(first, c % bps == 0)
    nzero = jnp.logical_and(last, c % bps == bps - 1)

    def glu(sl, row, nrows, lt):
        a = ring_ref[2 * sl, lt, pl.ds(row, nrows), :].astype(F32)
        b = ring_ref[2 * sl + 1, lt, pl.ds(row, nrows), :].astype(F32)
        return a * jax.nn.sigmoid(b)

    for lt in range(nl):
        lanes = slice(lt * LANES, (lt + 1) * LANES)
        zr_ref[0, 0:CONV_HALO, lanes] = jnp.where(pzero, 0.0, glu(pslot, prow, CONV_HALO, lt))
        zr_ref[0, CONV_HALO:CONV_HALO + sr, lanes] = glu(slot, row0, sr, lt)
        zr_ref[0, CONV_HALO + sr:ext, lanes] = jnp.where(nzero, 0.0, glu(nslot, nrow, CONV_HALO, lt))
    for sft in range(1, SUBLANES):
        zr_ref[sft, 0:ext - SUBLANES, :] = zr_ref[0, sft:sft + ext - SUBLANES, :]

    prev = [None] * CONV_CHAINS
    for lt in range(nl):
        lanes = slice(lt * LANES, (lt + 1) * LANES)
        wk = [wdw_ref[kk, :, lanes] for kk in range(CONV_WIDTH)]
        for g in range(sr // SUBLANES):
            ch = g % CONV_CHAINS
            acc = _zero_after(prev[ch]) if chain and prev[ch] is not None else None
            for kk in range(CONV_WIDTH):
                a, sft = divmod(CONV_HALO - CONV_PAD + kk, SUBLANES)
                term = zr_ref[sft, (g + a) * SUBLANES:(g + a + 1) * SUBLANES, lanes] * wk[kk]
                acc = term if acc is None else acc + term
            y_ref[g * SUBLANES:(g + 1) * SUBLANES, lanes] = acc
            prev[ch] = acc

    bdw = bdw_ref[...]
    lg = lg_ref[...]
    lb = lb_ref[...]
    for base in range(0, sr, NORM_CHUNK):
        y = y_ref[base:base + NORM_CHUNK, :] + bdw
        mu = jnp.mean(y, axis=-1, keepdims=True)
        yc = y - mu
        var = jnp.mean(yc * yc, axis=-1, keepdims=True)
        yn = yc * lax.rsqrt(var + EPS) * lg + lb
        zc_ref[base:base + NORM_CHUNK, :] = (yn * jax.nn.sigmoid(yn)).astype(BF16)


def _inproj_conv_kernel(x_ref, g_ref, w_ref, wdw_ref, bdw_ref, lg_ref, lb_ref, o_ref, zc_ref,
                        u_ref, ring_ref, zr_ref, y_ref, *, n_blocks, bps, n_j, q_scale):
    i = pl.program_id(0)
    j = pl.program_id(1)
    ca_j, cb_j = n_j - 3, n_j - 2
    n_sub = n_j - 1
    in_grid = i < n_blocks

    @pl.when(jnp.logical_and(in_grid, j == 0))
    def _():
        x = x_ref[...]
        ms = jnp.mean(x * x, axis=-1, keepdims=True)
        u_ref[...] = (x * lax.rsqrt(ms + EPS) * g_ref[...]).astype(BF16)

    @pl.when(jnp.logical_and(i == 0, j == 0))
    def _():
        ring_ref[2:4] = jnp.zeros((2,) + ring_ref.shape[1:], BF16)

    def project():
        acc = jnp.dot(u_ref[...], w_ref[...], preferred_element_type=F32)
        acc = acc * jnp.where(j == 0, q_scale, 1.0).astype(F32)
        for s in range(o_ref.shape[0]):
            o_ref[s] = acc[:, s * LANES:(s + 1) * LANES].astype(BF16)

    def conv(chain):
        s = jnp.where(j == n_j - 1, n_sub - 1, jnp.minimum(j, n_sub - 2))
        _conv_sub_block(ring_ref, wdw_ref, bdw_ref, lg_ref, lb_ref, zc_ref, zr_ref, y_ref,
                        c=i - 1, s=s, n_sub=n_sub, bps=bps, chain=chain)

    @pl.when(jnp.logical_and(in_grid, jnp.logical_or(i == 0, j == cb_j)))
    def _():
        project()

    @pl.when(jnp.logical_and(jnp.logical_and(i >= 1, in_grid), j != cb_j))
    def _():
        conv(chain=True)
        project()

    @pl.when(jnp.logical_and(i == n_blocks, j != cb_j))
    def _():
        conv(chain=False)

    @pl.when(jnp.logical_and(in_grid, jnp.logical_or(j == ca_j, j == cb_j)))
    def _():
        ring_ref[2 * (i % 2) + (j - ca_j)] = o_ref[...]


def _inproj_conv(x2, g, w, w_dw, b_dw, ln_g, ln_b, s, d_attn, d_conv):
    t, d = x2.shape
    n = w.shape[1]
    tm = _pick(s, 1024)
    tn = d_attn
    assert d_conv == tn and tn % LANES == 0 and n % tn == 0 and t % tm == 0
    nl = tn // LANES
    n_j = n // tn
    n_sub = n_j - 1
    ca_tile = 3 * d_attn // tn
    assert tm % n_sub == 0
    sr = tm // n_sub
    assert sr % NORM_CHUNK == 0 and sr % CONV_HALO == 0
    n_blocks = t // tm
    rest = [c for c in range(n_j) if c not in (ca_tile, ca_tile + 1)]
    order = rest[:-1] + [ca_tile, ca_tile + 1] + rest[-1:]
    assert order[0] == 0

    def tile(i, j):
        return jnp.where(i == n_blocks, order[-1], _select(j, order))

    def blk(i):
        return jnp.minimum(i, n_blocks - 1)

    def zc_index(i, j):
        sub = jnp.where(j == n_j - 1, n_sub - 1, jnp.minimum(j, n_sub - 2))
        return jnp.where(i == 0, 0, (i - 1) * n_sub + sub)

    def vec():
        return pl.BlockSpec((1, d_conv), lambda i, j: (0, 0))

    w_b = jnp.broadcast_to(w_dw[:, None, :], (CONV_WIDTH, SUBLANES, d_conv))
    kern = functools.partial(_inproj_conv_kernel, n_blocks=n_blocks, bps=s // tm, n_j=n_j,
                             q_scale=HEAD_DIM ** -0.5 * LOG2E)
    return pl.pallas_call(
        kern,
        out_shape=(jax.ShapeDtypeStruct((n // LANES, t, LANES), BF16), jax.ShapeDtypeStruct((t, d_conv), BF16)),
        grid=(n_blocks + 1, n_j),
        in_specs=[
            pl.BlockSpec((tm, d), lambda i, j: (blk(i), 0)),
            pl.BlockSpec((1, d), lambda i, j: (0, 0)),
            pl.BlockSpec((d, tn), lambda i, j: (0, tile(i, j))),
            pl.BlockSpec((CONV_WIDTH, SUBLANES, d_conv), lambda i, j: (0, 0, 0)),
            vec(), vec(), vec(),
        ],
        out_specs=(pl.BlockSpec((nl, tm, LANES), lambda i, j: (tile(i, j), blk(i), 0)),
                   pl.BlockSpec((sr, d_conv), lambda i, j: (zc_index(i, j), 0))),
        scratch_shapes=[pltpu.VMEM((tm, d), BF16),
                        pltpu.VMEM((4, nl, tm, LANES), BF16),
                        pltpu.VMEM((SUBLANES, sr + 2 * CONV_HALO, d_conv), F32),
                        pltpu.VMEM((sr, d_conv), F32)],
        compiler_params=_params("arbitrary", "arbitrary"),
        name="inproj_conv",
    )(x2, g.reshape(1, d), w, w_b, b_dw.reshape(1, -1), ln_g.reshape(1, -1), ln_b.reshape(1, -1))


def _attn_block_types(rows, r):
    kwr = r + WIN_R - 1
    kwr += (kwr * GRID_W) % LANES // GRID_W
    kwr = min(kwr, rows)
    nb = rows // r
    interior = [i for i in range(nb) if WIN_R // 2 <= i * r <= min(rows - kwr + WIN_R // 2,
                                                                  rows - r - (WIN_R // 2 - 1))]
    assert interior and interior == list(range(interior[0], interior[-1] + 1))
    n_top = interior[0]
    n_bot = nb - 1 - interior[-1]
    type_r0 = [interior[0] * r] + [i * r for i in range(n_top)] + [i * r for i in range(nb - n_bot, nb)]
    return kwr, nb, n_top, n_bot, type_r0


def _attn_bias(rpb, rows, r):
    kwr, _, _, _, type_r0 = _attn_block_types(rows, r)
    kr_sz = min(WIN_R, rows)
    kc_sz = min(WIN_C, GRID_W)
    heads, n_dr, _ = rpb.shape
    rpb = rpb.astype(F32) * LOG2E
    pad = jnp.full((heads, n_dr, GRID_W), MASK_BIAS, F32)
    padded = jnp.concatenate([pad, rpb, pad], axis=-1)
    colb = jnp.stack([padded[:, :, GRID_W + WIN_C - 1 - c:2 * GRID_W + WIN_C - 1 - c] for c in range(GRID_W)],
                     axis=2)
    col = np.arange(GRID_W)
    cs = np.clip(col - kc_sz // 2, 0, GRID_W - kc_sz)
    col_ok = (col[None, :] >= cs[:, None]) & (col[None, :] < cs[:, None] + kc_sz)
    colb = jnp.where(col_ok[None, None], colb, MASK_BIAS)
    masked = jnp.full((heads, GRID_W, GRID_W), MASK_BIAS, F32)
    types = []
    for r0 in type_r0:
        ws = int(np.clip(r0 - kr_sz // 2, 0, rows - kwr))
        q_rows = []
        for a in range(r):
            qr = r0 + a
            rs = int(np.clip(qr - kr_sz // 2, 0, rows - kr_sz))
            blocks = []
            for wi in range(kwr):
                kr = ws + wi
                blocks.append(colb[:, kr - qr + WIN_R - 1] if rs <= kr < rs + kr_sz else masked)
            q_rows.append(jnp.concatenate(blocks, axis=2))
        types.append(jnp.concatenate(q_rows, axis=1))
    return jnp.stack(types, axis=1)


def _attn_kernel(q_ref, k_ref, v_ref, b_ref, o_ref, *, rows, r, kwr, sub, nb, n_top, n_bot):
    step = pl.program_id(2)
    rq = r * GRID_W
    kw = kwr * GRID_W
    for sb in range(sub):
        i = step * sub + sb
        ws = jnp.clip(i * r - WIN_R // 2, 0, rows - kwr)
        bt = jnp.where(i < n_top, 1 + i, jnp.where(i >= nb - n_bot, 1 + n_top + i - (nb - n_bot), 0))
        start = pl.multiple_of(ws * GRID_W, GRID_W)
        q = q_ref[0, sb * rq:(sb + 1) * rq, :]
        k = k_ref[0, pl.ds(start, kw), :]
        v = v_ref[0, pl.ds(start, kw), :]
        s = lax.dot_general(q, k, (((1,), (1,)), ((), ())), preferred_element_type=F32)
        s = s + b_ref[0, bt]
        m = jnp.max(s, axis=-1, keepdims=True)
        p = jnp.exp2(s - m)
        l = jnp.sum(p, axis=-1, keepdims=True)
        o = jnp.dot(p.astype(BF16), v, preferred_element_type=F32)
        o_ref[0, sb * rq:(sb + 1) * rq, :] = (o / l).astype(BF16)


def _attention(proj_s, rpb, b, s, d_attn):
    heads = d_attn // HEAD_DIM
    rows = s // GRID_W
    r = ATTN_ROWS if rows % ATTN_ROWS == 0 and rows >= 2 * WIN_R else 1
    kwr, nb, n_top, n_bot, type_r0 = _attn_block_types(rows, r)
    sub = _pick(nb, ATTN_SUB)
    bias = _attn_bias(rpb, rows, r)
    rq, kw = r * GRID_W, kwr * GRID_W
    steps = nb // sub
    kern = functools.partial(_attn_kernel, rows=rows, r=r, kwr=kwr, sub=sub, nb=nb, n_top=n_top, n_bot=n_bot)
    return pl.pallas_call(
        kern,
        out_shape=jax.ShapeDtypeStruct((b, s, d_attn), BF16),
        grid=(b, heads, steps),
        in_specs=[
            pl.BlockSpec((1, rq * sub, HEAD_DIM), lambda bi, h, i: (h, bi * steps + i, 0)),
            pl.BlockSpec((1, s, HEAD_DIM), lambda bi, h, i: (heads + h, bi, 0)),
            pl.BlockSpec((1, s, HEAD_DIM), lambda bi, h, i: (2 * heads + h, bi, 0)),
            pl.BlockSpec((1, len(type_r0), rq, kw), lambda bi, h, i: (h, 0, 0, 0)),
        ],
        out_specs=pl.BlockSpec((1, rq * sub, HEAD_DIM), lambda bi, h, i: (bi, i, h)),
        compiler_params=_params("parallel", "parallel", "arbitrary"),
        name="nattn",
    )(proj_s, proj_s, proj_s, bias)


def _merge_kernel(*refs, n_gate):
    at_ref, zc_ref = refs[:2]
    ga_refs = refs[2:2 + n_gate]
    gb_refs = refs[2 + n_gate:2 + 2 * n_gate]
    x_ref, wa_ref, wc_ref, wo_ref, g_ref, gn_ref, o_ref, u_ref, raw_ref = refs[2 + 2 * n_gate:]
    ns = ga_refs[0].shape[0]
    s_id = pl.program_id(0)
    n = pl.num_programs(0) - 1
    tm = x_ref.shape[0]

    def project():
        ya = jnp.dot(at_ref[...], wa_ref[...], preferred_element_type=F32)
        yc = jnp.dot(zc_ref[...], wc_ref[...], preferred_element_type=F32)
        parts = []
        for p in range(n_gate):
            for s in range(ns):
                c0 = (p * ns + s) * LANES
                ga = jax.nn.sigmoid(ga_refs[p][s].astype(F32))
                gb = jax.nn.sigmoid(gb_refs[p][s].astype(F32))
                parts.append((ga * ya[:, c0:c0 + LANES] + gb * yc[:, c0:c0 + LANES]).astype(BF16))
        merged = jnp.concatenate(parts, axis=-1)
        raw_ref[s_id % 2] = jnp.dot(merged, wo_ref[...], preferred_element_type=F32)

    def finish(chain):
        slot = (s_id + 1) % 2
        g = g_ref[...]
        gn = gn_ref[...]
        prev = None
        for r0 in range(0, tm, MERGE_CHUNK):
            o = raw_ref[slot, r0:r0 + MERGE_CHUNK, :]
            ms = jnp.mean(o * o, axis=-1, keepdims=True)
            if chain and prev is not None:
                ms = ms + _zero_after(prev)
            h = x_ref[r0:r0 + MERGE_CHUNK, :] + o * lax.rsqrt(ms + EPS) * g
            o_ref[r0:r0 + MERGE_CHUNK, :] = h
            hs = jnp.mean(h * h, axis=-1, keepdims=True)
            u_ref[r0:r0 + MERGE_CHUNK, :] = (h * lax.rsqrt(hs + EPS) * gn).astype(BF16)
            prev = hs

    @pl.when(s_id == 0)
    def _():
        project()

    @pl.when(jnp.logical_and(s_id > 0, s_id < n))
    def _():
        finish(chain=True)
        project()

    @pl.when(s_id == n)
    def _():
        finish(chain=False)


def _merge(attn2, zc2, proj_s, x2, w_attn_o, w_conv_o, w_out, g, g_next, slab_ga):
    t, d = x2.shape
    ns = math.gcd(slab_ga, d // LANES)
    n_gate = d // LANES // ns
    blk = slab_ga // ns
    da = attn2.shape[1]
    dc = zc2.shape[1]
    tm = _pick(t, 256)
    assert tm % MERGE_CHUNK == 0
    n = t // tm

    def cur(i):
        return jnp.minimum(i, n - 1)

    def done(i):
        return jnp.maximum(i - 1, 0)

    def wspec(shape):
        return pl.BlockSpec(shape, lambda i: (0, 0), pipeline_mode=pl.Buffered(1))

    def gspec(p):
        return pl.BlockSpec((ns, tm, LANES), lambda i: (blk + p, cur(i), 0))

    return pl.pallas_call(
        functools.partial(_merge_kernel, n_gate=n_gate),
        out_shape=(jax.ShapeDtypeStruct((t, d), F32), jax.ShapeDtypeStruct((t, d), BF16)),
        grid=(n + 1,),
        in_specs=[
            pl.BlockSpec((tm, da), lambda i: (cur(i), 0)),
            pl.BlockSpec((tm, dc), lambda i: (cur(i), 0)),
            *[gspec(p) for p in range(2 * n_gate)],
            pl.BlockSpec((tm, d), lambda i: (done(i), 0)),
            wspec((da, d)), wspec((dc, d)), wspec((d, d)),
            pl.BlockSpec((1, d), lambda i: (0, 0)),
            pl.BlockSpec((1, d), lambda i: (0, 0)),
        ],
        out_specs=(pl.BlockSpec((tm, d), lambda i: (done(i), 0)), pl.BlockSpec((tm, d), lambda i: (done(i), 0))),
        scratch_shapes=[pltpu.VMEM((2, tm, d), F32)],
        compiler_params=_params("arbitrary"),
        name="merge",
    )(attn2, zc2, *([proj_s] * (2 * n_gate)), x2, w_attn_o, w_conv_o, w_out, g.reshape(1, d),
      g_next.reshape(1, d))


def _ffn_up_kernel(u_ref, wg_ref, wu_ref, o_ref):
    u = u_ref[...]
    gate = jnp.dot(u, wg_ref[...], preferred_element_type=F32)
    up = jnp.dot(u, wu_ref[...], preferred_element_type=F32)
    o_ref[...] = (gate * jax.nn.sigmoid(gate) * up).astype(BF16)


def _ffn_up(u2, w_gate_up):
    t, d = u2.shape
    dff = w_gate_up.shape[1] // 2
    tm = _pick(t, 1024)
    tf = _pick(dff, 512)
    nf = dff // tf
    return pl.pallas_call(
        _ffn_up_kernel,
        out_shape=jax.ShapeDtypeStruct((t, dff), BF16),
        grid=(t // tm, nf),
        in_specs=[
            pl.BlockSpec((tm, d), lambda i, j: (i, 0)),
            pl.BlockSpec((d, tf), lambda i, j: (0, j)),
            pl.BlockSpec((d, tf), lambda i, j: (0, nf + j)),
        ],
        out_specs=pl.BlockSpec((tm, tf), lambda i, j: (i, j)),
        compiler_params=_params("parallel", "arbitrary"),
        name="ffn_up",
    )(u2, w_gate_up, w_gate_up)


def _ffn_down_kernel(a_ref, w_ref, h_ref, g_ref, o_ref):
    f = jnp.dot(a_ref[...], w_ref[...], preferred_element_type=F32)
    ms = jnp.mean(f * f, axis=-1, keepdims=True)
    o_ref[...] = h_ref[...] + f * lax.rsqrt(ms + EPS) * g_ref[...]


def _ffn_down(act, w_down, h2, g):
    t, d = h2.shape
    dff = act.shape[1]
    tm = _pick(t, FFN_DOWN_ROWS)
    return pl.pallas_call(
        _ffn_down_kernel,
        out_shape=jax.ShapeDtypeStruct((t, d), F32),
        grid=(t // tm,),
        in_specs=[
            pl.BlockSpec((tm, dff), lambda i: (i, 0)),
            pl.BlockSpec((dff, d), lambda i: (0, 0), pipeline_mode=pl.Buffered(1)),
            pl.BlockSpec((tm, d), lambda i: (i, 0)),
            pl.BlockSpec((1, d), lambda i: (0, 0)),
        ],
        out_specs=pl.BlockSpec((tm, d), lambda i: (i, 0)),
        compiler_params=_params("parallel"),
        name="ffn_down",
    )(act, w_down, h2, g.reshape(1, d))


def kernel(x, mix_pre_g, mix_post_g, w_in, rpb, w_attn_o, w_dw, b_dw, conv_ln_g, conv_ln_b, w_conv_o, w_out,
           ffn_pre_g, ffn_post_g, w_gate_up, w_down):
    b, s, d = x.shape
    t = b * s
    depth = w_in.shape[0]
    d_attn = w_attn_o.shape[1]
    d_conv = w_conv_o.shape[1]
    assert d_attn == d_conv and w_in.shape[2] == 3 * d_attn + 2 * d_conv + 2 * d and d == 2 * d_conv
    h = x.reshape(t, d)
    for l in range(depth):
        proj_s, zc = _inproj_conv(h, mix_pre_g[l], w_in[l].astype(BF16), w_dw[l], b_dw[l], conv_ln_g[l],
                                  conv_ln_b[l], s, d_attn, d_conv)
        attn = _attention(proj_s, rpb[l], b, s, d_attn)
        h, u2 = _merge(attn.reshape(t, d_attn), zc.reshape(t, d_conv), proj_s, h, w_attn_o[l].astype(BF16),
                       w_conv_o[l].astype(BF16), w_out[l].astype(BF16), mix_post_g[l], ffn_pre_g[l],
                       (3 * d_attn + 2 * d_conv) // LANES)
        act = _ffn_up(u2, w_gate_up[l].astype(BF16))
        h = _ffn_down(act, w_down[l].astype(BF16), h, ffn_post_g[l])
    return h.reshape(b, s, d)
```

```python
import functools
import math

import jax
import jax.numpy as jnp
import numpy as np
from jax import lax
from jax.experimental import pallas as pl
from jax.experimental.pallas import tpu as pltpu

GRID_W = 64
WIN_R = 8
WIN_C = 16
HEAD_DIM = 128
CONV_WIDTH = 31
CONV_PAD = CONV_WIDTH // 2
EPS = 1e-6
MASK_BIAS = -1e30
LOG2E = math.log2(math.e)

LANES = 128
SUBLANES = 8
VMEM_LIMIT_BYTES = 56 * 1024 * 1024

ATTN_ROWS = 4
ATTN_SUB = 16
FFN_DOWN_ROWS = 512
CONV_HALO = 16
NORM_CHUNK = 16
CONV_CHAINS = 2

BF16 = jnp.bfloat16
F32 = jnp.float32


def _params(*sem, flags=None):
    return pltpu.CompilerParams(dimension_semantics=sem, vmem_limit_bytes=VMEM_LIMIT_BYTES, flags=flags)


def _pick(n, pref):
    t = min(pref, n)
    while n % t:
        t //= 2
    return t


def _zero_after(v):
    bits = lax.bitcast_convert_type(v, jnp.uint32)
    return lax.bitcast_convert_type((bits >> 16) >> 16, F32)


def _select(j, values):
    out = values[-1]
    for idx in range(len(values) - 2, -1, -1):
        out = jnp.where(j == idx, values[idx], out)
    return out


def _conv_sub_block(ring_ref, wdw_ref, bdw_ref, lg_ref, lb_ref, zc_ref, zr_ref, y_ref, *, c, s, n_sub, bps,
                    chain):
    nl, tm = ring_ref.shape[1], ring_ref.shape[2]
    sr = tm // n_sub
    ext = sr + 2 * CONV_HALO
    slot = c % 2
    first = s == 0
    last = s == n_sub - 1
    row0 = pl.multiple_of(s * sr, sr)
    pslot = jnp.where(first, 1 - slot, slot)
    prow = pl.multiple_of(jnp.where(first, tm - CONV_HALO, row0 - CONV_HALO), CONV_HALO)
    nslot = jnp.where(last, 1 - slot, slot)
    nrow = pl.multiple_of(jnp.where(last, 0, row0 + sr), CONV_HALO)
    pzero = jnp.logical_and(first, c % bps == 0)
    nzero = jnp.logical_and(last, c % bps == bps - 1)

    def glu(sl, row, nrows, lt):
        a = ring_ref[2 * sl, lt, pl.ds(row, nrows), :].astype(F32)
        b = ring_ref[2 * sl + 1, lt, pl.ds(row, nrows), :].astype(F32)
        return a * jax.nn.sigmoid(b)

    for lt in range(nl):
        lanes = slice(lt * LANES, (lt + 1) * LANES)
        zr_ref[0, 0:CONV_HALO, lanes] = jnp.where(pzero, 0.0, glu(pslot, prow, CONV_HALO, lt))
        zr_ref[0, CONV_HALO:CONV_HALO + sr, lanes] = glu(slot, row0, sr, lt)
        zr_ref[0, CONV_HALO + sr:ext, lanes] = jnp.where(nzero, 0.0, glu(nslot, nrow, CONV_HALO, lt))
    for sft in range(1, SUBLANES):
        zr_ref[sft, 0:ext - SUBLANES, :] = zr_ref[0, sft:sft + ext - SUBLANES, :]

    prev = [None] * CONV_CHAINS
    for lt in range(nl):
        lanes = slice(lt * LANES, (lt + 1) * LANES)
        wk = [wdw_ref[kk, :, lanes] for kk in range(CONV_WIDTH)]
        for g in range(sr // SUBLANES):
            ch = g % CONV_CHAINS
            acc = _zero_after(prev[ch]) if chain and prev[ch] is not None else None
            for kk in range(CONV_WIDTH):
                a, sft = divmod(CONV_HALO - CONV_PAD + kk, SUBLANES)
                term = zr_ref[sft, (g + a) * SUBLANES:(g + a + 1) * SUBLANES, lanes] * wk[kk]
                acc = term if acc is None else acc + term
            y_ref[g * SUBLANES:(g + 1) * SUBLANES, lanes] = acc
            prev[ch] = acc

    bdw = bdw_ref[...]
    lg = lg_ref[...]
    lb = lb_ref[...]
    for base in range(0, sr, NORM_CHUNK):
        y = y_ref[base:base + NORM_CHUNK, :] + bdw
        mu = jnp.mean(y, axis=-1, keepdims=True)
        yc = y - mu
        var = jnp.mean(yc * yc, axis=-1, keepdims=True)
        yn = yc * lax.rsqrt(var + EPS) * lg + lb
        zc_ref[base:base + NORM_CHUNK, :] = (yn * jax.nn.sigmoid(yn)).astype(BF16)


def _inproj_conv_kernel(x_ref, g_ref, w_ref, wdw_ref, bdw_ref, lg_ref, lb_ref, o_ref, zc_ref,
                        u_ref, ring_ref, zr_ref, y_ref, *, n_blocks, bps, n_j, q_scale):
    i = pl.program_id(0)
    j = pl.program_id(1)
    ca_j, cb_j = n_j - 3, n_j - 2
    n_sub = n_j - 1
    in_grid = i < n_blocks

    @pl.when(jnp.logical_and(in_grid, j == 0))
    def _():
        x = x_ref[...]
        ms = jnp.mean(x * x, axis=-1, keepdims=True)
        u_ref[...] = (x * lax.rsqrt(ms + EPS) * g_ref[...]).astype(BF16)

    @pl.when(jnp.logical_and(i == 0, j == 0))
    def _():
        ring_ref[2:4] = jnp.zeros((2,) + ring_ref.shape[1:], BF16)

    def project():
        acc = jnp.dot(u_ref[...], w_ref[...], preferred_element_type=F32)
        acc = acc * jnp.where(j == 0, q_scale, 1.0).astype(F32)
        for s in range(o_ref.shape[0]):
            o_ref[s] = acc[:, s * LANES:(s + 1) * LANES].astype(BF16)

    def conv(chain):
        s = jnp.where(j == n_j - 1, n_sub - 1, jnp.minimum(j, n_sub - 2))
        _conv_sub_block(ring_ref, wdw_ref, bdw_ref, lg_ref, lb_ref, zc_ref, zr_ref, y_ref,
                        c=i - 1, s=s, n_sub=n_sub, bps=bps, chain=chain)

    @pl.when(jnp.logical_and(in_grid, jnp.logical_or(i == 0, j == cb_j)))
    def _():
        project()

    @pl.when(jnp.logical_and(jnp.logical_and(i >= 1, in_grid), j != cb_j))
    def _():
        conv(chain=True)
        project()

    @pl.when(jnp.logical_and(i == n_blocks, j != cb_j))
    def _():
        conv(chain=False)

    @pl.when(jnp.logical_and(in_grid, jnp.logical_or(j == ca_j, j == cb_j)))
    def _():
        ring_ref[2 * (i % 2) + (j - ca_j)] = o_ref[...]


def _inproj_conv(x2, g, w, w_dw, b_dw, ln_g, ln_b, s, d_attn, d_conv):
    t, d = x2.shape
    n = w.shape[1]
    tm = _pick(s, 1024)
    tn = d_attn
    assert d_conv == tn and tn % LANES == 0 and n % tn == 0 and t % tm == 0
    nl = tn // LANES
    n_j = n // tn
    n_sub = n_j - 1
    ca_tile = 3 * d_attn // tn
    assert tm % n_sub == 0
    sr = tm // n_sub
    assert sr % NORM_CHUNK == 0 and sr % CONV_HALO == 0
    n_blocks = t // tm
    rest = [c for c in range(n_j) if c not in (ca_tile, ca_tile + 1)]
    order = rest[:-1] + [ca_tile, ca_tile + 1] + rest[-1:]
    assert order[0] == 0

    def tile(i, j):
        return jnp.where(i == n_blocks, order[-1], _select(j, order))

    def blk(i):
        return jnp.minimum(i, n_blocks - 1)

    def zc_index(i, j):
        sub = jnp.where(j == n_j - 1, n_sub - 1, jnp.minimum(j, n_sub - 2))
        return jnp.where(i == 0, 0, (i - 1) * n_sub + sub)

    def vec():
        return pl.BlockSpec((1, d_conv), lambda i, j: (0, 0))

    w_b = jnp.broadcast_to(w_dw[:, None, :], (CONV_WIDTH, SUBLANES, d_conv))
    kern = functools.partial(_inproj_conv_kernel, n_blocks=n_blocks, bps=s // tm, n_j=n_j,
                             q_scale=HEAD_DIM ** -0.5 * LOG2E)
    return pl.pallas_call(
        kern,
        out_shape=(jax.ShapeDtypeStruct((n // LANES, t, LANES), BF16), jax.ShapeDtypeStruct((t, d_conv), BF16)),
        grid=(n_blocks + 1, n_j),
        in_specs=[
            pl.BlockSpec((tm, d), lambda i, j: (blk(i), 0)),
            pl.BlockSpec((1, d), lambda i, j: (0, 0)),
            pl.BlockSpec((d, tn), lambda i, j: (0, tile(i, j))),
            pl.BlockSpec((CONV_WIDTH, SUBLANES, d_conv), lambda i, j: (0, 0, 0)),
            vec(), vec(), vec(),
        ],
        out_specs=(pl.BlockSpec((nl, tm, LANES), lambda i, j: (tile(i, j), blk(i), 0)),
                   pl.BlockSpec((sr, d_conv), lambda i, j: (zc_index(i, j), 0))),
        scratch_shapes=[pltpu.VMEM((tm, d), BF16),
                        pltpu.VMEM((4, nl, tm, LANES), BF16),
                        pltpu.VMEM((SUBLANES, sr + 2 * CONV_HALO, d_conv), F32),
                        pltpu.VMEM((sr, d_conv), F32)],
        compiler_params=_params("arbitrary", "arbitrary"),
        name="inproj_conv",
    )(x2, g.reshape(1, d), w, w_b, b_dw.reshape(1, -1), ln_g.reshape(1, -1), ln_b.reshape(1, -1))


def _attn_block_types(rows, r):
    kwr = r + WIN_R - 1
    kwr += (kwr * GRID_W) % LANES // GRID_W
    kwr = min(kwr, rows)
    nb = rows // r
    interior = [i for i in range(nb) if WIN_R // 2 <= i * r <= min(rows - kwr + WIN_R // 2,
                                                                  rows - r - (WIN_R // 2 - 1))]
    assert interior and interior == list(range(interior[0], interior[-1] + 1))
    n_top = interior[0]
    n_bot = nb - 1 - interior[-1]
    type_r0 = [interior[0] * r] + [i * r for i in range(n_top)] + [i * r for i in range(nb - n_bot, nb)]
    return kwr, nb, n_top, n_bot, type_r0


def _attn_bias(rpb, rows, r):
    kwr, _, _, _, type_r0 = _attn_block_types(rows, r)
    kr_sz = min(WIN_R, rows)
    kc_sz = min(WIN_C, GRID_W)
    heads, n_dr, _ = rpb.shape
    rpb = rpb.astype(F32) * LOG2E
    pad = jnp.full((heads, n_dr, GRID_W), MASK_BIAS, F32)
    padded = jnp.concatenate([pad, rpb, pad], axis=-1)
    colb = jnp.stack([padded[:, :, GRID_W + WIN_C - 1 - c:2 * GRID_W + WIN_C - 1 - c] for c in range(GRID_W)],
                     axis=2)
    col = np.arange(GRID_W)
    cs = np.clip(col - kc_sz // 2, 0, GRID_W - kc_sz)
    col_ok = (col[None, :] >= cs[:, None]) & (col[None, :] < cs[:, None] + kc_sz)
    colb = jnp.where(col_ok[None, None], colb, MASK_BIAS)
    masked = jnp.full((heads, GRID_W, GRID_W), MASK_BIAS, F32)
    types = []
    for r0 in type_r0:
        ws = int(np.clip(r0 - kr_sz // 2, 0, rows - kwr))
        q_rows = []
        for a in range(r):
            qr = r0 + a
            rs = int(np.clip(qr - kr_sz // 2, 0, rows - kr_sz))
            blocks = []
            for wi in range(kwr):
                kr = ws + wi
                blocks.append(colb[:, kr - qr + WIN_R - 1] if rs <= kr < rs + kr_sz else masked)
            q_rows.append(jnp.concatenate(blocks, axis=2))
        types.append(jnp.concatenate(q_rows, axis=1))
    return jnp.stack(types, axis=1)


def _attn_kernel(q_ref, k_ref, v_ref, b_ref, o_ref, *, rows, r, kwr, sub, nb, n_top, n_bot):
    step = pl.program_id(2)
    rq = r * GRID_W
    kw = kwr * GRID_W
    for sb in range(sub):
        i = step * sub + sb
        ws = jnp.clip(i * r - WIN_R // 2, 0, rows - kwr)
        bt = jnp.where(i < n_top, 1 + i, jnp.where(i >= nb - n_bot, 1 + n_top + i - (nb - n_bot), 0))
        start = pl.multiple_of(ws * GRID_W, GRID_W)
        q = q_ref[0, sb * rq:(sb + 1) * rq, :]
        k = k_ref[0, pl.ds(start, kw), :]
        v = v_ref[0, pl.ds(start, kw), :]
        s = lax.dot_general(q, k, (((1,), (1,)), ((), ())), preferred_element_type=F32)
        s = s + b_ref[0, bt]
        m = jnp.max(s, axis=-1, keepdims=True)
        p = jnp.exp2(s - m)
        l = jnp.sum(p, axis=-1, keepdims=True)
        o = jnp.dot(p.astype(BF16), v, preferred_element_type=F32)
        o_ref[0, sb * rq:(sb + 1) * rq, :] = (o / l).astype(BF16)


def _attention(proj_s, rpb, b, s, d_attn):
    heads = d_attn // HEAD_DIM
    rows = s // GRID_W
    r = ATTN_ROWS if rows % ATTN_ROWS == 0 and rows >= 2 * WIN_R else 1
    kwr, nb, n_top, n_bot, type_r0 = _attn_block_types(rows, r)
    sub = _pick(nb, ATTN_SUB)
    bias = _attn_bias(rpb, rows, r)
    rq, kw = r * GRID_W, kwr * GRID_W
    steps = nb // sub
    kern = functools.partial(_attn_kernel, rows=rows, r=r, kwr=kwr, sub=sub, nb=nb, n_top=n_top, n_bot=n_bot)
    return pl.pallas_call(
        kern,
        out_shape=jax.ShapeDtypeStruct((b, s, d_attn), BF16),
        grid=(b, heads, steps),
        in_specs=[
            pl.BlockSpec((1, rq * sub, HEAD_DIM), lambda bi, h, i: (h, bi * steps + i, 0)),
            pl.BlockSpec((1, s, HEAD_DIM), lambda bi, h, i: (heads + h, bi, 0)),
            pl.BlockSpec((1, s, HEAD_DIM), lambda bi, h, i: (2 * heads + h, bi, 0)),
            pl.BlockSpec((1, len(type_r0), rq, kw), lambda bi, h, i: (h, 0, 0, 0)),
        ],
        out_specs=pl.BlockSpec((1, rq * sub, HEAD_DIM), lambda bi, h, i: (bi, i, h)),
        compiler_params=_params("parallel", "parallel", "arbitrary"),
        name="nattn",
    )(proj_s, proj_s, proj_s, bias)


def _merge_kernel(*refs, n_gate):
    at_ref, zc_ref = refs[:2]
    ga_refs = refs[2:2 + n_gate]
    gb_refs = refs[2 + n_gate:2 + 2 * n_gate]
    x_ref, wa_ref, wc_ref, wo_ref, g_ref, gn_ref, o_ref, u_ref = refs[2 + 2 * n_gate:]
    ns = ga_refs[0].shape[0]
    ya = jnp.dot(at_ref[...], wa_ref[...], preferred_element_type=F32)
    yc = jnp.dot(zc_ref[...], wc_ref[...], preferred_element_type=F32)
    parts = []
    for p in range(n_gate):
        for s in range(ns):
            c0 = (p * ns + s) * LANES
            ga = jax.nn.sigmoid(ga_refs[p][s].astype(F32))
            gb = jax.nn.sigmoid(gb_refs[p][s].astype(F32))
            parts.append((ga * ya[:, c0:c0 + LANES] + gb * yc[:, c0:c0 + LANES]).astype(BF16))
    merged = jnp.concatenate(parts, axis=-1)
    o = jnp.dot(merged, wo_ref[...], preferred_element_type=F32)
    ms = jnp.mean(o * o, axis=-1, keepdims=True)
    h = x_ref[...] + o * lax.rsqrt(ms + EPS) * g_ref[...]
    o_ref[...] = h
    hs = jnp.mean(h * h, axis=-1, keepdims=True)
    u_ref[...] = (h * lax.rsqrt(hs + EPS) * gn_ref[...]).astype(BF16)


def _merge(attn2, zc2, proj_s, x2, w_attn_o, w_conv_o, w_out, g, g_next, slab_ga):
    t, d = x2.shape
    ns = math.gcd(slab_ga, d // LANES)
    n_gate = d // LANES // ns
    blk = slab_ga // ns
    da = attn2.shape[1]
    dc = zc2.shape[1]
    tm = _pick(t, 256)

    def wspec(shape):
        return pl.BlockSpec(shape, lambda i: (0, 0), pipeline_mode=pl.Buffered(1))

    def gspec(p):
        return pl.BlockSpec((ns, tm, LANES), lambda i: (blk + p, i, 0))

    return pl.pallas_call(
        functools.partial(_merge_kernel, n_gate=n_gate),
        out_shape=(jax.ShapeDtypeStruct((t, d), F32), jax.ShapeDtypeStruct((t, d), BF16)),
        grid=(t // tm,),
        in_specs=[
            pl.BlockSpec((tm, da), lambda i: (i, 0)),
            pl.BlockSpec((tm, dc), lambda i: (i, 0)),
            *[gspec(p) for p in range(2 * n_gate)],
            pl.BlockSpec((tm, d), lambda i: (i, 0)),
            wspec((da, d)), wspec((dc, d)), wspec((d, d)),
            pl.BlockSpec((1, d), lambda i: (0, 0)),
            pl.BlockSpec((1, d), lambda i: (0, 0)),
        ],
        out_specs=(pl.BlockSpec((tm, d), lambda i: (i, 0)), pl.BlockSpec((tm, d), lambda i: (i, 0))),
        compiler_params=_params("parallel"),
        name="merge",
    )(attn2, zc2, *([proj_s] * (2 * n_gate)), x2, w_attn_o, w_conv_o, w_out, g.reshape(1, d),
      g_next.reshape(1, d))


def _ffn_up_kernel(u_ref, wg_ref, wu_ref, o_ref):
    u = u_ref[...]
    gate = jnp.dot(u, wg_ref[...], preferred_element_type=F32)
    up = jnp.dot(u, wu_ref[...], preferred_element_type=F32)
    o_ref[...] = (gate * jax.nn.sigmoid(gate) * up).astype(BF16)


def _ffn_up(u2, w_gate_up):
    t, d = u2.shape
    dff = w_gate_up.shape[1] // 2
    tm = _pick(t, 1024)
    tf = _pick(dff, 512)
    nf = dff // tf
    return pl.pallas_call(
        _ffn_up_kernel,
        out_shape=jax.ShapeDtypeStruct((t, dff), BF16),
        grid=(t // tm, nf),
        in_specs=[
            pl.BlockSpec((tm, d), lambda i, j: (i, 0)),
            pl.BlockSpec((d, tf), lambda i, j: (0, j)),
            pl.BlockSpec((d, tf), lambda i, j: (0, nf + j)),
        ],
        out_specs=pl.BlockSpec((tm, tf), lambda i, j: (i, j)),
        compiler_params=_params("parallel", "arbitrary"),
        name="ffn_up",
    )(u2, w_gate_up, w_gate_up)


def _ffn_down_kernel(a_ref, w_ref, h_ref, g_ref, o_ref):
    f = jnp.dot(a_ref[...], w_ref[...], preferred_element_type=F32)
    ms = jnp.mean(f * f, axis=-1, keepdims=True)
    o_ref[...] = h_ref[...] + f * lax.rsqrt(ms + EPS) * g_ref[...]


def _ffn_down(act, w_down, h2, g):
    t, d = h2.shape
    dff = act.shape[1]
    tm = _pick(t, FFN_DOWN_ROWS)
    return pl.pallas_call(
        _ffn_down_kernel,
        out_shape=jax.ShapeDtypeStruct((t, d), F32),
        grid=(t // tm,),
        in_specs=[
            pl.BlockSpec((tm, dff), lambda i: (i, 0)),
            pl.BlockSpec((dff, d), lambda i: (0, 0), pipeline_mode=pl.Buffered(1)),
            pl.BlockSpec((tm, d), lambda i: (i, 0)),
            pl.BlockSpec((1, d), lambda i: (0, 0)),
        ],
        out_specs=pl.BlockSpec((tm, d), lambda i: (i, 0)),
        compiler_params=_params("parallel"),
        name="ffn_down",
    )(act, w_down, h2, g.reshape(1, d))


def kernel(x, mix_pre_g, mix_post_g, w_in, rpb, w_attn_o, w_dw, b_dw, conv_ln_g, conv_ln_b, w_conv_o, w_out,
           ffn_pre_g, ffn_post_g, w_gate_up, w_down):
    b, s, d = x.shape
    t = b * s
    depth = w_in.shape[0]
    d_attn = w_attn_o.shape[1]
    d_conv = w_conv_o.shape[1]
    assert d_attn == d_conv and w_in.shape[2] == 3 * d_attn + 2 * d_conv + 2 * d and d == 2 * d_conv
    h = x.reshape(t, d)
    for l in range(depth):
        proj_s, zc = _inproj_conv(h, mix_pre_g[l], w_in[l].astype(BF16), w_dw[l], b_dw[l], conv_ln_g[l],
                                  conv_ln_b[l], s, d_attn, d_conv)
        attn = _attention(proj_s, rpb[l], b, s, d_attn)
        h, u2 = _merge(attn.reshape(t, d_attn), zc.reshape(t, d_conv), proj_s, h, w_attn_o[l].astype(BF16),
                       w_conv_o[l].astype(BF16), w_out[l].astype(BF16), mix_post_g[l], ffn_pre_g[l],
                       (3 * d_attn + 2 * d_conv) // LANES)
        act = _ffn_up(u2, w_gate_up[l].astype(BF16))
        h = _ffn_down(act, w_down[l].astype(BF16), h, ffn_post_g[l])
    return h.reshape(b, s, d)
```
